```python
import math
import jax, jax.numpy as jnp
from jax import lax
import numpy as np

D_MODEL = 1024
BATCH = 8
SEQ = 2048
DEPTH = 2
DEC_BATCH = 128
DEC_SEQ = 1
PAST_LEN = 2048
PAGE_SIZE = 128

D_MIX = D_MODEL
ATTN_WIDTH = D_MIX // 2
CONV_CH = D_MIX - ATTN_WIDTH
HEAD_DIM = 64
N_HEADS = ATTN_WIDTH // (2 * HEAD_DIM)
N_SUB = 2 * N_HEADS
V_DIM = 2 * HEAD_DIM
ROT_DIM = HEAD_DIM // 4
ROPE_THETA = 500000.0
CONV_W = 31
CONV_BUF = CONV_W - 1
N_GROUPS = 4
EXP_PER_GROUP = 4
N_EXPERTS = N_GROUPS * EXP_PER_GROUP
TOP_K = 2
D_EXPERT = D_MODEL // 2
Q_BLOCK = 128
EPS = 1e-6
D_IN = 3 * ATTN_WIDTH + 2 * CONV_CH

kernel_name = 'hymba_diffattn_conformer_hiermoe_step'


def rmsnorm(x, g):
    xf = x.astype(jnp.float32)
    y = xf * lax.rsqrt(jnp.mean(xf * xf, axis=-1, keepdims=True) + EPS)
    return (y * g.astype(jnp.float32)).astype(x.dtype)


def layernorm(x, g, b):
    xf = x.astype(jnp.float32)
    mu = jnp.mean(xf, axis=-1, keepdims=True)
    xc = xf - mu
    y = xc * lax.rsqrt(jnp.mean(xc * xc, axis=-1, keepdims=True) + EPS)
    return (y * g.astype(jnp.float32) + b.astype(jnp.float32)).astype(x.dtype)


def rope_partial(x, pos):
    inv = jnp.power(ROPE_THETA, -jnp.arange(0, ROT_DIM, 2, dtype=jnp.float32) / ROT_DIM)
    ang = pos.astype(jnp.float32)[:, None] * inv[None, :]
    cos = jnp.cos(ang)[None, :, None, :]
    sin = jnp.sin(ang)[None, :, None, :]
    xr = x[..., :ROT_DIM].astype(jnp.float32)
    x1, x2 = xr[..., :ROT_DIM // 2], xr[..., ROT_DIM // 2:]
    rot = jnp.concatenate([x1 * cos - x2 * sin, x2 * cos + x1 * sin], axis=-1)
    return jnp.concatenate([rot.astype(x.dtype), x[..., ROT_DIM:]], axis=-1)


def project(h, w_in, q_g, k_g, pos):
    B, T, _ = h.shape
    z = h @ w_in
    q, k, v, a, gt = jnp.split(z, [ATTN_WIDTH, 2 * ATTN_WIDTH, 3 * ATTN_WIDTH, 3 * ATTN_WIDTH + CONV_CH], axis=-1)
    q = rope_partial(rmsnorm(q.reshape(B, T, N_SUB, HEAD_DIM), q_g), pos)
    k = rope_partial(rmsnorm(k.reshape(B, T, N_SUB, HEAD_DIM), k_g), pos)
    v = v.reshape(B, T, N_HEADS, V_DIM)
    u = a * jax.nn.sigmoid(gt)
    return q, k, v, u


def diff_lambda(lq1, lk1, lq2, lk2, lam_init):
    f = jnp.float32
    return (jnp.exp(jnp.sum(lq1.astype(f) * lk1.astype(f))) -
            jnp.exp(jnp.sum(lq2.astype(f) * lk2.astype(f))) + lam_init)


def diff_attend(q, k, v, q_pos, k_pos, lam):
    s = jnp.einsum('bqhd,bkhd->bhqk', q, k, preferred_element_type=jnp.float32) * (HEAD_DIM ** -0.5)
    s = jnp.where(k_pos[None, :] <= q_pos[:, None], s, -jnp.inf)
    p = jax.nn.softmax(s, axis=-1)
    B, _, Tq, Tk = p.shape
    p = p.reshape(B, N_HEADS, 2, Tq, Tk)
    a = (p[:, :, 0] - lam * p[:, :, 1]).astype(v.dtype)
    return jnp.einsum('bhqk,bkhe->bqhe', a, v)


def prompt_attention(q, k, v, lam):
    B, S = q.shape[0], q.shape[1]
    nb = S // Q_BLOCK
    qb = q.reshape(B, nb, Q_BLOCK, N_SUB, HEAD_DIM).transpose(1, 0, 2, 3, 4)
    k_pos = jnp.arange(S, dtype=jnp.int32)

    def blk(args):
        qi, i = args
        q_pos = i * Q_BLOCK + jnp.arange(Q_BLOCK, dtype=jnp.int32)
        return diff_attend(qi, k, v, q_pos, k_pos, lam)

    o = lax.map(blk, (qb, jnp.arange(nb, dtype=jnp.int32)))
    return o.transpose(1, 0, 2, 3, 4).reshape(B, S, N_HEADS, V_DIM)


def diff_finish(o, sub_g, lam_init):
    B, T = o.shape[0], o.shape[1]
    return (rmsnorm(o, sub_g) * (1.0 - lam_init)).reshape(B, T, ATTN_WIDTH)


def conv_branch(u_ext, w, b, ln_g, ln_b):
    y = lax.conv_general_dilated(u_ext, w[:, None, :], window_strides=(1,), padding='VALID',
                                 dimension_numbers=('NWC', 'WIO', 'NWC'),
                                 feature_group_count=CONV_CH) + b
    return jax.nn.silu(layernorm(y, ln_g, ln_b))


def hier_moe(h, w_rg, w_re, w_gate, w_up, w_down):
    B, T, D = h.shape
    x = h.reshape(B * T, D)
    gp = jax.nn.softmax((x @ w_rg).astype(jnp.float32), axis=-1)
    g_idx = jnp.argmax(gp, axis=-1)
    g_w = jnp.max(gp, axis=-1)
    el = (x @ w_re).astype(jnp.float32).reshape(-1, N_GROUPS, EXP_PER_GROUP)
    el = jnp.take_along_axis(el, g_idx[:, None, None], axis=1)[:, 0]
    tv, ti = lax.top_k(el, TOP_K)
    tw = jax.nn.softmax(tv, axis=-1) * g_w[:, None]
    eidx = g_idx[:, None] * EXP_PER_GROUP + ti
    gates = jnp.sum(jax.nn.one_hot(eidx, N_EXPERTS, dtype=jnp.float32) * tw[..., None], axis=1)
    hg = jnp.einsum('nd,edf->nef', x, w_gate)
    hu = jnp.einsum('nd,edf->nef', x, w_up)
    act = jax.nn.silu(hg) * hu * gates[..., None].astype(x.dtype)
    y = jnp.einsum('nef,efd->nd', act, w_down)
    return y.reshape(B, T, D)


def setup_inputs(seed: int = 0) -> dict:
    key = jax.random.key(seed)
    ks = jax.random.split(key, 32)
    f = jnp.float32
    n_pages = PAST_LEN // PAGE_SIZE
    n_phys = (DEC_BATCH * n_pages * 5) // 4
    nrm = lambda k, shape, s: jax.random.normal(k, shape, f) * s
    perm = jax.random.permutation(ks[5], n_phys)[:DEC_BATCH * n_pages]
    return {
        'x_prompt': nrm(ks[0], (BATCH, SEQ, D_MODEL), 1.0),
        'x_sample': nrm(ks[1], (DEC_BATCH, DEC_SEQ, D_MODEL), 1.0),
        'cache_k': nrm(ks[2], (DEPTH, n_phys, PAGE_SIZE, N_SUB, HEAD_DIM), 1.0),
        'cache_v': nrm(ks[3], (DEPTH, n_phys, PAGE_SIZE, N_HEADS, V_DIM), 1.0),
        'state_conv': nrm(ks[4], (DEPTH, DEC_BATCH, CONV_BUF, CONV_CH), 0.5),
        'page_table': perm.reshape(DEC_BATCH, n_pages).astype(jnp.int32),
        'norm1_g': 1.0 + nrm(ks[6], (DEPTH, D_MODEL), 0.02),
        'w_in': nrm(ks[7], (DEPTH, D_MODEL, D_IN), D_MODEL ** -0.5),
        'q_norm_g': 1.0 + nrm(ks[8], (DEPTH, HEAD_DIM), 0.02),
        'k_norm_g': 1.0 + nrm(ks[9], (DEPTH, HEAD_DIM), 0.02),
        'lam_q1': nrm(ks[10], (DEPTH, HEAD_DIM), 0.1),
        'lam_k1': nrm(ks[11], (DEPTH, HEAD_DIM), 0.1),
        'lam_q2': nrm(ks[12], (DEPTH, HEAD_DIM), 0.1),
        'lam_k2': nrm(ks[13], (DEPTH, HEAD_DIM), 0.1),
        'subln_g': 1.0 + nrm(ks[14], (DEPTH, V_DIM), 0.02),
        'conv_w': nrm(ks[15], (DEPTH, CONV_W, CONV_CH), CONV_W ** -0.5),
        'conv_b': nrm(ks[16], (DEPTH, CONV_CH), 0.02),
        'conv_ln_g': 1.0 + nrm(ks[17], (DEPTH, CONV_CH), 0.02),
        'conv_ln_b': nrm(ks[18], (DEPTH, CONV_CH), 0.02),
        'w_out': nrm(ks[19], (DEPTH, D_MIX, D_MODEL), D_MIX ** -0.5),
        'norm2_g': 1.0 + nrm(ks[20], (DEPTH, D_MODEL), 0.02),
        'w_router_group': nrm(ks[21], (DEPTH, D_MODEL, N_GROUPS), D_MODEL ** -0.5),
        'w_router_expert': nrm(ks[22], (DEPTH, D_MODEL, N_EXPERTS), D_MODEL ** -0.5),
        'w_gate': nrm(ks[23], (DEPTH, N_EXPERTS, D_MODEL, D_EXPERT), D_MODEL ** -0.5),
        'w_up': nrm(ks[24], (DEPTH, N_EXPERTS, D_MODEL, D_EXPERT), D_MODEL ** -0.5),
        'w_down': nrm(ks[25], (DEPTH, N_EXPERTS, D_EXPERT, D_MODEL), D_EXPERT ** -0.5),
    }


def reference(x_prompt, x_sample, cache_k, cache_v, state_conv, page_table,
              norm1_g, w_in, q_norm_g, k_norm_g, lam_q1, lam_k1, lam_q2, lam_k2, subln_g,
              conv_w, conv_b, conv_ln_g, conv_ln_b, w_out, norm2_g,
              w_router_group, w_router_expert, w_gate, w_up, w_down):
    B, S, D = x_prompt.shape
    DB, T = x_sample.shape[0], x_sample.shape[1]
    past = page_table.shape[1] * cache_k.shape[2]
    pos_p = jnp.arange(S, dtype=jnp.int32)
    pos_s = past + jnp.arange(T, dtype=jnp.int32)
    kpos_s = jnp.arange(past + T, dtype=jnp.int32)
    xp, xs = x_prompt, x_sample
    kp_l, vp_l, cp_l, ks_l, vs_l, cs_l = [], [], [], [], [], []
    for l in range(DEPTH):
        lam_init = 0.8 - 0.6 * math.exp(-0.3 * l)
        lam = diff_lambda(lam_q1[l], lam_k1[l], lam_q2[l], lam_k2[l], lam_init)

        h = rmsnorm(xp, norm1_g[l])
        q, k, v, u = project(h, w_in[l], q_norm_g[l], k_norm_g[l], pos_p)
        att = diff_finish(prompt_attention(q, k, v, lam), subln_g[l], lam_init)
        u_ext = jnp.pad(u, ((0, 0), (CONV_BUF, 0), (0, 0)))
        cnv = conv_branch(u_ext, conv_w[l], conv_b[l], conv_ln_g[l], conv_ln_b[l])
        xp = xp + jnp.concatenate([att, cnv], axis=-1) @ w_out[l]
        xp = xp + hier_moe(rmsnorm(xp, norm2_g[l]), w_router_group[l], w_router_expert[l],
                           w_gate[l], w_up[l], w_down[l])
        kp_l.append(k); vp_l.append(v); cp_l.append(u_ext[:, -CONV_BUF:])

        h = rmsnorm(xs, norm1_g[l])
        q, k, v, u = project(h, w_in[l], q_norm_g[l], k_norm_g[l], pos_s)
        k_past = cache_k[l, page_table].reshape(DB, past, N_SUB, HEAD_DIM)
        v_past = cache_v[l, page_table].reshape(DB, past, N_HEADS, V_DIM)
        k_all = jnp.concatenate([k_past, k], axis=1)
        v_all = jnp.concatenate([v_past, v], axis=1)
        att = diff_finish(diff_attend(q, k_all, v_all, pos_s, kpos_s, lam), subln_g[l], lam_init)
        u_ext = jnp.concatenate([state_conv[l], u], axis=1)
        cnv = conv_branch(u_ext, conv_w[l], conv_b[l], conv_ln_g[l], conv_ln_b[l])
        xs = xs + jnp.concatenate([att, cnv], axis=-1) @ w_out[l]
        xs = xs + hier_moe(rmsnorm(xs, norm2_g[l]), w_router_group[l], w_router_expert[l],
                           w_gate[l], w_up[l], w_down[l])
        ks_l.append(k); vs_l.append(v); cs_l.append(u_ext[:, -CONV_BUF:])

    return (xp, xs, jnp.stack(kp_l), jnp.stack(vp_l), jnp.stack(cp_l),
            jnp.stack(ks_l), jnp.stack(vs_l), jnp.stack(cs_l))
```

```python
import functools
import math

import jax
import jax.numpy as jnp
from jax import lax
from jax.experimental import pallas as pl
from jax.experimental.pallas import tpu as pltpu

F32 = jnp.float32
BF16 = jnp.bfloat16

D_MODEL = 1024
HEAD_DIM = 64
N_SUB = 8
N_HEADS = 4
V_DIM = 128
ATTN_WIDTH = 512
CONV_CH = 512
ROT_DIM = 16
ROPE_THETA = 500000.0
CONV_W = 31
CONV_BUF = CONV_W - 1
N_GROUPS = 4
EXP_PER_GROUP = 4
N_EXPERTS = 16
D_EXPERT = 512
EPS = 1e-6
D_IN = 3 * ATTN_WIDTH + 2 * CONV_CH

LANES = 128
HALO = 32
VMEM_LIMIT = 56 * 1024 * 1024


def _cparams(n_axes):
    return pltpu.CompilerParams(dimension_semantics=("arbitrary",) * n_axes,
                                vmem_limit_bytes=VMEM_LIMIT)


def _inproj_kernel(x_ref, g1_ref, w_ref, bd_ref, qg_ref, kg_ref, c_ref, s1_ref, s2_ref,
                   q_ref, k_ref, v_ref, u_ref, kb_ref, vb_ref):
    x = x_ref[...]
    ms = jnp.mean(x * x, axis=-1, keepdims=True)
    h = (x * lax.rsqrt(ms + EPS) * g1_ref[...]).astype(BF16)
    z = jnp.dot(h, w_ref[...], preferred_element_type=F32)
    c, s1, s2 = c_ref[...], s1_ref[...], s2_ref[...]
    bd = bd_ref[...]

    def head_norm_rope(t, g):
        hms = jnp.dot((t * t).astype(BF16), bd, preferred_element_type=F32) * (1.0 / HEAD_DIM)
        tn = t * lax.rsqrt(hms + EPS) * g
        outs = []
        for j in range(ATTN_WIDTH // LANES):
            blk = tn[:, j * LANES:(j + 1) * LANES]
            up = pltpu.roll(blk, LANES - ROT_DIM // 2, 1)
            dn = pltpu.roll(blk, ROT_DIM // 2, 1)
            outs.append(blk * c + up * s1 + dn * s2)
        return outs

    qs = head_norm_rope(z[:, 0:ATTN_WIDTH], qg_ref[...])
    ks = head_norm_rope(z[:, ATTN_WIDTH:2 * ATTN_WIDTH], kg_ref[...])
    for j in range(ATTN_WIDTH // LANES):
        sl = slice(j * LANES, (j + 1) * LANES)
        q_ref[:, sl] = (qs[j] * (HEAD_DIM ** -0.5)).astype(BF16)
        k_ref[:, sl] = ks[j]
        kb_ref[:, sl] = ks[j].astype(BF16)
    v = z[:, 2 * ATTN_WIDTH:3 * ATTN_WIDTH]
    v_ref[...] = v
    vb_ref[...] = v.astype(BF16)
    a = z[:, 3 * ATTN_WIDTH:3 * ATTN_WIDTH + CONV_CH]
    gt = z[:, 3 * ATTN_WIDTH + CONV_CH:]
    u_ref[...] = a * jax.nn.sigmoid(gt)


def _inproj(x, g1, w_in_b, bd, qg, kg, ct, s1t, s2t, tm):
    n = x.shape[0]
    npos = ct.shape[0] // tm
    row = lambda i: (i, 0)
    full = lambda i: (0, 0)
    pos = lambda i: (i % npos, 0)
    outs = pl.pallas_call(
        _inproj_kernel,
        grid=(n // tm,),
        in_specs=[
            pl.BlockSpec((tm, D_MODEL), row),
            pl.BlockSpec((1, D_MODEL), full),
            pl.BlockSpec((D_MODEL, D_IN), full),
            pl.BlockSpec((ATTN_WIDTH, ATTN_WIDTH), full),
            pl.BlockSpec((1, ATTN_WIDTH), full),
            pl.BlockSpec((1, ATTN_WIDTH), full),
            pl.BlockSpec((tm, LANES), pos),
            pl.BlockSpec((tm, LANES), pos),
            pl.BlockSpec((tm, LANES), pos),
        ],
        out_specs=[pl.BlockSpec((tm, ATTN_WIDTH), row)] * 6,
        out_shape=[
            jax.ShapeDtypeStruct((n, ATTN_WIDTH), BF16),
            jax.ShapeDtypeStruct((n, ATTN_WIDTH), F32),
            jax.ShapeDtypeStruct((n, ATTN_WIDTH), F32),
            jax.ShapeDtypeStruct((n, CONV_CH), F32),
            jax.ShapeDtypeStruct((n, ATTN_WIDTH), BF16),
            jax.ShapeDtypeStruct((n, ATTN_WIDTH), BF16),
        ],
        compiler_params=_cparams(1),
        name="inproj",
    )(x, g1, w_in_b, bd, qg, kg, ct, s1t, s2t)
    return outs


def _lambda_value(lam_ref, lam_init):
    lp = lam_ref[...]
    t1 = jnp.sum(lp[0:1] * lp[1:2], axis=1, keepdims=True)
    t2 = jnp.sum(lp[2:3] * lp[3:4], axis=1, keepdims=True)
    return jnp.exp(t1) - jnp.exp(t2) + lam_init


def _subln(o, g, lam_init):
    ms = jnp.mean(o * o, axis=-1, keepdims=True)
    return o * lax.rsqrt(ms + EPS) * g * (1.0 - lam_init)


def _pattn_kernel(q_ref, k_ref, v_ref, lam_ref, sg_ref, o_ref, m_scr, l_scr, acc_scr, *, tq, lam_init):
    i = pl.program_id(1)
    lam = _lambda_value(lam_ref, lam_init)
    sg = sg_ref[...]
    lane = lax.broadcasted_iota(jnp.int32, (tq, LANES), 1)
    rows = lax.broadcasted_iota(jnp.int32, (2 * tq, tq), 0)
    cols = lax.broadcasted_iota(jnp.int32, (2 * tq, tq), 1)
    causal = cols <= jnp.where(rows >= tq, rows - tq, rows)
    nrep = tq // LANES

    for hd in range(N_HEADS):
        sl = slice(hd * LANES, (hd + 1) * LANES)
        qp = q_ref[:, sl]
        zero = jnp.zeros_like(qp)
        qs = jnp.concatenate([jnp.where(lane < HEAD_DIM, qp, zero),
                              jnp.where(lane >= HEAD_DIM, qp, zero)], axis=0)
        m_scr[...] = jnp.full(m_scr.shape, -jnp.inf, F32)
        l_scr[...] = jnp.zeros(l_scr.shape, F32)
        acc_scr[...] = jnp.zeros(acc_scr.shape, F32)

        def step(j, masked):
            off = pl.multiple_of(j * tq, tq)
            kj = k_ref[pl.ds(off, tq), sl]
            vj = v_ref[pl.ds(off, tq), sl]
            s = lax.dot_general(qs, kj, (((1,), (1,)), ((), ())), preferred_element_type=F32)
            if masked:
                s = jnp.where(causal, s, -jnp.inf)
            m_prev = m_scr[...]
            m_new = jnp.maximum(m_prev, jnp.max(s, axis=1, keepdims=True))
            alpha = jnp.exp(m_prev - m_new)
            p = jnp.exp(s - jnp.concatenate([m_new] * nrep, axis=1))
            l_scr[...] = alpha * l_scr[...] + jnp.sum(p, axis=1, keepdims=True)
            acc_scr[...] = alpha * acc_scr[...] + jnp.dot(p.astype(BF16), vj, preferred_element_type=F32)
            m_scr[...] = m_new

        def body(j, carry):
            step(j, False)
            return carry

        lax.fori_loop(0, i, body, 0)
        step(i, True)

        o = acc_scr[...] / l_scr[...]
        od = o[0:tq] - lam * o[tq:2 * tq]
        o_ref[:, sl] = _subln(od, sg, lam_init).astype(BF16)


def _prompt_attention(q, kb, vb, lam_p, sg, batch, seq, tq, lam_init):
    nq = seq // tq
    n = batch * seq
    kern = functools.partial(_pattn_kernel, tq=tq, lam_init=lam_init)
    return pl.pallas_call(
        kern,
        grid=(batch, nq),
        in_specs=[
            pl.BlockSpec((tq, ATTN_WIDTH), lambda b, i: (b * nq + i, 0)),
            pl.BlockSpec((seq, ATTN_WIDTH), lambda b, i: (b, 0)),
            pl.BlockSpec((seq, ATTN_WIDTH), lambda b, i: (b, 0)),
            pl.BlockSpec((4, HEAD_DIM), lambda b, i: (0, 0)),
            pl.BlockSpec((1, V_DIM), lambda b, i: (0, 0)),
        ],
        out_specs=pl.BlockSpec((tq, ATTN_WIDTH), lambda b, i: (b * nq + i, 0)),
        out_shape=jax.ShapeDtypeStruct((n, ATTN_WIDTH), BF16),
        scratch_shapes=[pltpu.VMEM((2 * tq, LANES), F32)] * 3,
        compiler_params=_cparams(2),
        name="prompt_attn",
    )(q, kb, vb, lam_p, sg)


def _dattn_kernel(pt_ref, q_ref, kn_ref, vn_ref, lam_ref, sg_ref, *rest, n_pages, page, lam_init):
    k_refs = rest[:n_pages]
    v_refs = rest[n_pages:2 * n_pages]
    o_ref = rest[2 * n_pages]
    del pt_ref
    lam = _lambda_value(lam_ref, lam_init)
    q3 = q_ref[...]
    s = jnp.concatenate([jnp.sum(k_refs[p][...] * q3, axis=1) for p in range(n_pages)], axis=1)
    s_self = jnp.sum(q3 * kn_ref[...], axis=1)
    m = jnp.maximum(jnp.max(s, axis=1, keepdims=True), s_self)
    pexp = jnp.exp(s - m)
    pself = jnp.exp(s_self - m)
    inv = 1.0 / (jnp.sum(pexp, axis=1, keepdims=True) + pself)
    sub = lax.broadcasted_iota(jnp.int32, (N_SUB, 1), 0)
    coef = jnp.where((sub & 1) == 0, 1.0, -lam) * inv
    a8 = pexp * coef
    a8 = (a8 + pltpu.roll(a8, N_SUB - 1, 0)).astype(BF16)
    as8 = jnp.broadcast_to(pself * coef, (N_SUB, V_DIM))
    as8 = as8 + pltpu.roll(as8, N_SUB - 1, 0)
    o_self = as8 * vn_ref[...]
    rows = []
    for hd in range(N_HEADS):
        acc = jnp.zeros((N_SUB, V_DIM), F32)
        for p in range(n_pages):
            v_hd = v_refs[p][pl.ds(hd, page, stride=N_HEADS), :].astype(BF16)
            acc = acc + jnp.dot(a8[:, p * page:(p + 1) * page], v_hd, preferred_element_type=F32)
        rows.append(acc[2 * hd:2 * hd + 1] + o_self[2 * hd:2 * hd + 1])
    o = jnp.concatenate(rows, axis=0)
    o_ref[...] = _subln(o, sg_ref[...], lam_init)


def _decode_attention(page_table, q3, kn3, vn8, lam_p, sg, ckt, cvf, layer, lam_init):
    db, n_pages = page_table.shape
    page = ckt.shape[-1]
    kern = functools.partial(_dattn_kernel, n_pages=n_pages, page=page, lam_init=lam_init)
    const2 = lambda b, pt: (0, 0)

    def kmap(p):
        return lambda b, pt: (layer, pt[b, p], 0, 0, 0)

    def vmap_(p):
        return lambda b, pt: (layer, pt[b, p], 0, 0)

    in_specs = [
        pl.BlockSpec((None, N_SUB, HEAD_DIM, 1), lambda b, pt: (b, 0, 0, 0)),
        pl.BlockSpec((None, N_SUB, HEAD_DIM, 1), lambda b, pt: (b, 0, 0, 0)),
        pl.BlockSpec((None, N_SUB, V_DIM), lambda b, pt: (b, 0, 0)),
        pl.BlockSpec((4, HEAD_DIM), const2),
        pl.BlockSpec((1, V_DIM), const2),
    ]
    in_specs += [pl.BlockSpec((None, None, N_SUB, HEAD_DIM, page), kmap(p)) for p in range(n_pages)]
    in_specs += [pl.BlockSpec((None, None, page * N_HEADS, V_DIM), vmap_(p)) for p in range(n_pages)]
    grid_spec = pltpu.PrefetchScalarGridSpec(
        num_scalar_prefetch=1,
        grid=(db,),
        in_specs=in_specs,
        out_specs=pl.BlockSpec((None, N_HEADS, V_DIM), lambda b, pt: (b, 0, 0)),
    )
    return pl.pallas_call(
        kern,
        grid_spec=grid_spec,
        out_shape=jax.ShapeDtypeStruct((db, N_HEADS, V_DIM), F32),
        compiler_params=_cparams(1),
        name="decode_attn",
    )(page_table, q3, kn3, vn8, lam_p, sg, *([ckt] * n_pages), *([cvf] * n_pages))


def _conv_post(y, lg_ref, lb_ref):
    mu = jnp.mean(y, axis=-1, keepdims=True)
    yc = y - mu
    var = jnp.mean(yc * yc, axis=-1, keepdims=True)
    yn = yc * lax.rsqrt(var + EPS) * lg_ref[...] + lb_ref[...]
    return yn * jax.nn.sigmoid(yn)


def _mix_tail(x, att, cnv, wo_ref, g2_ref, w1_ref, w2_ref, xo_ref, h_ref, gate_ref):
    y = (x + jnp.dot(att, wo_ref[0:ATTN_WIDTH, :], preferred_element_type=F32)
         + jnp.dot(cnv.astype(BF16), wo_ref[ATTN_WIDTH:, :], preferred_element_type=F32))
    xo_ref[...] = y
    ms = jnp.mean(y * y, axis=-1, keepdims=True)
    h = y * lax.rsqrt(ms + EPS) * g2_ref[...]
    h_hi = h.astype(BF16)
    h_lo = (h - h_hi.astype(F32)).astype(BF16)
    h_ref[...] = h_hi
    r = jnp.dot(h_hi, w1_ref[...], preferred_element_type=F32)
    lg = r + pltpu.roll(r, LANES // 2, 1) + jnp.dot(h_lo, w2_ref[...], preferred_element_type=F32)
    tm = lg.shape[0]
    lane = lax.broadcasted_iota(jnp.int32, (tm, LANES), 1)
    lanef = lane.astype(F32)
    big = float(LANES)
    neg = -jnp.inf
    gmask = lane < N_GROUPS
    gl = jnp.where(gmask, lg, neg)
    gmax = jnp.max(gl, axis=1, keepdims=True)
    gsum = jnp.sum(jnp.where(gmask, jnp.exp(gl - gmax), 0.0), axis=1, keepdims=True)
    g_w = 1.0 / gsum
    g_idx = jnp.min(jnp.where(gl == gmax, lanef, big), axis=1, keepdims=True)
    lo = N_GROUPS + EXP_PER_GROUP * g_idx
    emask = (lanef >= lo) & (lanef < lo + EXP_PER_GROUP)
    ev = jnp.where(emask, lg, neg)
    v1 = jnp.max(ev, axis=1, keepdims=True)
    i1 = jnp.min(jnp.where(ev == v1, lanef, big), axis=1, keepdims=True)
    ev2 = jnp.where(lanef == i1, neg, ev)
    v2 = jnp.max(ev2, axis=1, keepdims=True)
    i2 = jnp.min(jnp.where(ev2 == v2, lanef, big), axis=1, keepdims=True)
    e2 = jnp.exp(v2 - v1)
    den = 1.0 / (1.0 + e2)
    w1 = den * g_w
    w2 = e2 * den * g_w
    gates = jnp.where(lanef == i1, w1, jnp.where(lanef == i2, w2, 0.0))
    gate_ref[...] = pltpu.roll(gates, LANES - N_GROUPS, 1)


def _mix_prompt_kernel(x_ref, att_ref, u_ref, uh_ref, cw_ref, cb_ref, lg_ref, lb_ref, wo_ref, g2_ref,
                       w1_ref, w2_ref, xo_ref, h_ref, gate_ref, ext_scr, *, tm, tiles_per_seq):
    i = pl.program_id(0)
    first = (i % tiles_per_seq) == 0
    ext_scr[0:HALO, :] = jnp.where(first, 0.0, uh_ref[...])
    ext_scr[HALO:, :] = u_ref[...]
    acc = jnp.zeros((tm, CONV_CH), F32) + cb_ref[...]
    for j in range(CONV_W):
        acc = acc + cw_ref[j:j + 1, :] * ext_scr[pl.ds(j + HALO - CONV_BUF, tm), :]
    cnv = _conv_post(acc, lg_ref, lb_ref)
    _mix_tail(x_ref[...], att_ref[...], cnv, wo_ref, g2_ref, w1_ref, w2_ref, xo_ref, h_ref, gate_ref)


def _mix_decode_kernel(x_ref, att_ref, u_ref, st_ref, cw_ref, cb_ref, lg_ref, lb_ref, wo_ref, g2_ref,
                       w1_ref, w2_ref, xo_ref, h_ref, gate_ref, ns_ref):
    u = u_ref[...]
    acc = cb_ref[...] + cw_ref[CONV_BUF:CONV_BUF + 1, :] * u
    for j in range(CONV_BUF):
        acc = acc + cw_ref[j:j + 1, :] * st_ref[j]
    for j in range(CONV_BUF - 1):
        ns_ref[j] = st_ref[j + 1]
    ns_ref[CONV_BUF - 1] = u
    cnv = _conv_post(acc, lg_ref, lb_ref)
    _mix_tail(x_ref[...], att_ref[...], cnv, wo_ref, g2_ref, w1_ref, w2_ref, xo_ref, h_ref, gate_ref)


def _mix_common_specs(tm):
    row = lambda i: (i, 0)
    full = lambda i: (0, 0)
    tail_in = [
        pl.BlockSpec((HALO, CONV_CH), full),
        pl.BlockSpec((1, CONV_CH), full),
        pl.BlockSpec((1, CONV_CH), full),
        pl.BlockSpec((1, CONV_CH), full),
        pl.BlockSpec((D_MODEL, D_MODEL), full),
        pl.BlockSpec((1, D_MODEL), full),
        pl.BlockSpec((D_MODEL, LANES), full),
        pl.BlockSpec((D_MODEL, LANES), full),
    ]
    out_specs = [pl.BlockSpec((tm, D_MODEL), row), pl.BlockSpec((tm, D_MODEL), row),
                 pl.BlockSpec((tm, LANES), row)]
    return row, tail_in, out_specs


def _mix_out_shape(n):
    return [jax.ShapeDtypeStruct((n, D_MODEL), F32), jax.ShapeDtypeStruct((n, D_MODEL), BF16),
            jax.ShapeDtypeStruct((n, LANES), F32)]


def _mix_prompt(x, att, u, cwp, cb, lg, lb, wo_b, g2, w1, w2, seq, tm):
    n = x.shape[0]
    row, tail_in, out_specs = _mix_common_specs(tm)
    halo_map = lambda i: (jnp.maximum(i * (tm // HALO) - 1, 0), 0)
    kern = functools.partial(_mix_prompt_kernel, tm=tm, tiles_per_seq=seq // tm)
    return pl.pallas_call(
        kern,
        grid=(n // tm,),
        in_specs=[pl.BlockSpec((tm, D_MODEL), row), pl.BlockSpec((tm, ATTN_WIDTH), row),
                  pl.BlockSpec((tm, CONV_CH), row), pl.BlockSpec((HALO, CONV_CH), halo_map)] + tail_in,
        out_specs=out_specs,
        out_shape=_mix_out_shape(n),
        scratch_shapes=[pltpu.VMEM((tm + HALO, CONV_CH), F32)],
        compiler_params=_cparams(1),
        name="mix_prompt",
    )(x, att, u, u, cwp, cb, lg, lb, wo_b, g2, w1, w2)


def _mix_decode(x, att, u, state_t, layer, cwp, cb, lg, lb, wo_b, g2, w1, w2, tm):
    n = x.shape[0]
    row, tail_in, out_specs = _mix_common_specs(tm)
    return pl.pallas_call(
        _mix_decode_kernel,
        grid=(n // tm,),
        in_specs=[pl.BlockSpec((tm, D_MODEL), row), pl.BlockSpec((tm, ATTN_WIDTH), row),
                  pl.BlockSpec((tm, CONV_CH), row),
                  pl.BlockSpec((None, CONV_BUF, tm, CONV_CH), lambda i: (layer, 0, i, 0))] + tail_in,
        out_specs=out_specs + [pl.BlockSpec((CONV_BUF, tm, CONV_CH), lambda i: (0, i, 0))],
        out_shape=_mix_out_shape(n) + [jax.ShapeDtypeStruct((CONV_BUF, n, CONV_CH), F32)],
        compiler_params=_cparams(1),
        name="mix_decode",
    )(x, att, u, state_t, cwp, cb, lg, lb, wo_b, g2, w1, w2)


def _moe_kernel(x_ref, h_ref, gate_ref, wg_ref, wu_ref, wd_ref, o_ref):
    e = pl.program_id(1)

    @pl.when(e == 0)
    def _():
        o_ref[...] = x_ref[...]

    h = h_ref[...]
    hg = jnp.dot(h, wg_ref[...], preferred_element_type=F32)
    hu = jnp.dot(h, wu_ref[...], preferred_element_type=F32)
    gates = gate_ref[...]
    lane = lax.broadcasted_iota(jnp.int32, gates.shape, 1)
    ge = jnp.sum(jnp.where(lane == e, gates, 0.0), axis=1, keepdims=True)
    act = (hg * jax.nn.sigmoid(hg)) * hu * ge
    o_ref[...] += jnp.dot(act.astype(BF16), wd_ref[...], preferred_element_type=F32)


def _moe(x, h, gates, wg_b, wu_b, wd_b, tm):
    n = x.shape[0]
    row = lambda i, e: (i, 0)
    return pl.pallas_call(
        _moe_kernel,
        grid=(n // tm, N_EXPERTS),
        in_specs=[
            pl.BlockSpec((tm, D_MODEL), row),
            pl.BlockSpec((tm, D_MODEL), row),
            pl.BlockSpec((tm, LANES), row),
            pl.BlockSpec((None, D_MODEL, D_EXPERT), lambda i, e: (e, 0, 0)),
            pl.BlockSpec((None, D_MODEL, D_EXPERT), lambda i, e: (e, 0, 0)),
            pl.BlockSpec((None, D_EXPERT, D_MODEL), lambda i, e: (e, 0, 0)),
        ],
        out_specs=pl.BlockSpec((tm, D_MODEL), row),
        out_shape=jax.ShapeDtypeStruct((n, D_MODEL), F32),
        compiler_params=_cparams(2),
        name="moe",
    )(x, h, gates, wg_b, wu_b, wd_b)


def _rope_tables(pos):
    half = ROT_DIM // 2
    inv = jnp.power(ROPE_THETA, -jnp.arange(0, ROT_DIM, 2, dtype=F32) / ROT_DIM)
    ang = pos.astype(F32)[:, None] * inv[None, :]
    cos, sin = jnp.cos(ang), jnp.sin(ang)
    n = pos.shape[0]
    pad = jnp.zeros((n, HEAD_DIM - ROT_DIM), F32)
    zer = jnp.zeros((n, half), F32)
    c = jnp.concatenate([cos, cos, pad + 1.0], axis=1)
    s1 = jnp.concatenate([-sin, zer, pad], axis=1)
    s2 = jnp.concatenate([zer, sin, pad], axis=1)
    rep = LANES // HEAD_DIM
    return jnp.tile(c, (1, rep)), jnp.tile(s1, (1, rep)), jnp.tile(s2, (1, rep))


def kernel(x_prompt, x_sample, cache_k, cache_v, state_conv, page_table, norm1_g, w_in, q_norm_g, k_norm_g,
           lam_q1, lam_k1, lam_q2, lam_k2, subln_g, conv_w, conv_b, conv_ln_g, conv_ln_b, w_out, norm2_g,
           w_router_group, w_router_expert, w_gate, w_up, w_down):
    batch, seq, d = x_prompt.shape
    db = x_sample.shape[0]
    depth = w_in.shape[0]
    n_pages, page = page_table.shape[1], cache_k.shape[2]
    past = n_pages * page
    n = batch * seq

    xp = x_prompt.reshape(n, d)
    xs = x_sample.reshape(db, d)
    tabs_p = _rope_tables(jnp.arange(seq, dtype=jnp.int32))
    tabs_s = _rope_tables(jnp.full((db,), past, dtype=jnp.int32))
    gi = jnp.arange(ATTN_WIDTH, dtype=jnp.int32) // HEAD_DIM
    bd = (gi[:, None] == gi[None, :]).astype(BF16)
    ckt = jnp.transpose(cache_k, (0, 1, 3, 4, 2))
    cvf = cache_v.reshape(depth, cache_v.shape[1], page * N_HEADS, V_DIM)
    state_t = jnp.transpose(state_conv, (0, 2, 1, 3))

    kp_l, vp_l, cp_l, ks_l, vs_l, cs_l = [], [], [], [], [], []
    for l in range(depth):
        lam_init = 0.8 - 0.6 * math.exp(-0.3 * l)
        g1 = norm1_g[l][None, :]
        w_in_b = w_in[l].astype(BF16)
        qg = jnp.tile(q_norm_g[l], N_SUB)[None, :]
        kg = jnp.tile(k_norm_g[l], N_SUB)[None, :]
        lam_p = jnp.stack([lam_q1[l], lam_k1[l], lam_q2[l], lam_k2[l]])
        sg = subln_g[l][None, :]
        cwp = jnp.pad(conv_w[l], ((0, HALO - CONV_W), (0, 0)))
        cb, lg, lb = conv_b[l][None, :], conv_ln_g[l][None, :], conv_ln_b[l][None, :]
        wo_b = w_out[l].astype(BF16)
        g2 = norm2_g[l][None, :]
        wr = jnp.concatenate([w_router_group[l], w_router_expert[l]], axis=1)
        wr_hi = wr.astype(BF16)
        wr_lo = (wr - wr_hi.astype(F32)).astype(BF16)
        nr = wr.shape[1]
        w1 = jnp.zeros((d, LANES), BF16).at[:, :nr].set(wr_hi).at[:, LANES // 2:LANES // 2 + nr].set(wr_lo)
        w2 = jnp.zeros((d, LANES), BF16).at[:, :nr].set(wr_hi)
        wg_b, wu_b, wd_b = w_gate[l].astype(BF16), w_up[l].astype(BF16), w_down[l].astype(BF16)

        q, k, v, u, kb, vb = _inproj(xp, g1, w_in_b, bd, qg, kg, *tabs_p, tm=512)
        att = _prompt_attention(q, kb, vb, lam_p, sg, batch, seq, 256, lam_init)
        xp, h2, gates = _mix_prompt(xp, att, u, cwp, cb, lg, lb, wo_b, g2, w1, w2, seq, 256)
        xp = _moe(xp, h2, gates, wg_b, wu_b, wd_b, 1024)
        kp_l.append(k.reshape(batch, seq, N_SUB, HEAD_DIM))
        vp_l.append(v.reshape(batch, seq, N_HEADS, V_DIM))
        cp_l.append(u.reshape(batch, seq, CONV_CH)[:, seq - CONV_BUF:])

        q, k, v, u, _, _ = _inproj(xs, g1, w_in_b, bd, qg, kg, *tabs_s, tm=db)
        q3 = q.astype(F32).reshape(db, N_SUB, HEAD_DIM, 1)
        kn3 = k.reshape(db, N_SUB, HEAD_DIM, 1)
        vn8 = jnp.repeat(v.reshape(db, N_HEADS, V_DIM), 2, axis=1)
        att = _decode_attention(page_table, q3, kn3, vn8, lam_p, sg, ckt, cvf, l, lam_init)
        att = att.reshape(db, ATTN_WIDTH).astype(BF16)
        xs, h2, gates, ns = _mix_decode(xs, att, u, state_t, l, cwp, cb, lg, lb, wo_b, g2, w1, w2, 64)
        xs = _moe(xs, h2, gates, wg_b, wu_b, wd_b, db)
        ks_l.append(k.reshape(db, 1, N_SUB, HEAD_DIM))
        vs_l.append(v.reshape(db, 1, N_HEADS, V_DIM))
        cs_l.append(jnp.transpose(ns, (1, 0, 2)))

    return (xp.reshape(batch, seq, d), xs.reshape(db, 1, d), jnp.stack(kp_l), jnp.stack(vp_l), jnp.stack(cp_l),
            jnp.stack(ks_l), jnp.stack(vs_l), jnp.stack(cs_l))
```

```python
import functools
import math

import jax
import jax.numpy as jnp
from jax import lax
from jax.experimental import pallas as pl
from jax.experimental.pallas import tpu as pltpu

F32 = jnp.float32
BF16 = jnp.bfloat16

D_MODEL = 1024
HEAD_DIM = 64
N_SUB = 8
N_HEADS = 4
V_DIM = 128
ATTN_WIDTH = 512
CONV_CH = 512
ROT_DIM = 16
ROPE_THETA = 500000.0
CONV_W = 31
CONV_BUF = CONV_W - 1
N_GROUPS = 4
EXP_PER_GROUP = 4
N_EXPERTS = 16
D_EXPERT = 512
EPS = 1e-6
D_IN = 3 * ATTN_WIDTH + 2 * CONV_CH

LANES = 128
SUBLANES = 8
HALO = 32
EXT_TAIL = 16
VMEM_LIMIT = 56 * 1024 * 1024


def _cparams(n_axes):
    return pltpu.CompilerParams(dimension_semantics=("arbitrary",) * n_axes,
                                vmem_limit_bytes=VMEM_LIMIT)


def _inproj_kernel(x_ref, g1_ref, w_ref, bd_ref, qg_ref, kg_ref, c_ref, s1_ref, s2_ref,
                   q_ref, k_ref, v_ref, u_ref, kb_ref):
    x = x_ref[...]
    ms = jnp.mean(x * x, axis=-1, keepdims=True)
    h = (x * lax.rsqrt(ms + EPS) * g1_ref[...]).astype(BF16)
    z = jnp.dot(h, w_ref[...], preferred_element_type=F32)
    c, s1, s2 = c_ref[...], s1_ref[...], s2_ref[...]
    bd = bd_ref[...]

    def head_norm_rope(t, g):
        hms = jnp.dot((t * t).astype(BF16), bd, preferred_element_type=F32) * (1.0 / HEAD_DIM)
        tn = t * lax.rsqrt(hms + EPS) * g
        outs = []
        for j in range(ATTN_WIDTH // LANES):
            blk = tn[:, j * LANES:(j + 1) * LANES]
            up = pltpu.roll(blk, LANES - ROT_DIM // 2, 1)
            dn = pltpu.roll(blk, ROT_DIM // 2, 1)
            outs.append(blk * c + up * s1 + dn * s2)
        return outs

    qs = head_norm_rope(z[:, 0:ATTN_WIDTH], qg_ref[...])
    ks = head_norm_rope(z[:, ATTN_WIDTH:2 * ATTN_WIDTH], kg_ref[...])
    for j in range(ATTN_WIDTH // LANES):
        sl = slice(j * LANES, (j + 1) * LANES)
        q_ref[:, sl] = (qs[j] * (HEAD_DIM ** -0.5)).astype(BF16)
        k_ref[:, sl] = ks[j]
        kb_ref[:, sl] = ks[j].astype(BF16)
    v_ref[...] = z[:, 2 * ATTN_WIDTH:3 * ATTN_WIDTH]
    a = z[:, 3 * ATTN_WIDTH:3 * ATTN_WIDTH + CONV_CH]
    gt = z[:, 3 * ATTN_WIDTH + CONV_CH:]
    u_ref[...] = a * jax.nn.sigmoid(gt)


def _inproj(x, g1, w_in_b, bd, qg, kg, ct, s1t, s2t, tm):
    n = x.shape[0]
    npos = ct.shape[0] // tm
    row = lambda i: (i, 0)
    full = lambda i: (0, 0)
    pos = lambda i: (i % npos, 0)
    outs = pl.pallas_call(
        _inproj_kernel,
        grid=(n // tm,),
        in_specs=[
            pl.BlockSpec((tm, D_MODEL), row),
            pl.BlockSpec((1, D_MODEL), full),
            pl.BlockSpec((D_MODEL, D_IN), full),
            pl.BlockSpec((ATTN_WIDTH, ATTN_WIDTH), full),
            pl.BlockSpec((1, ATTN_WIDTH), full),
            pl.BlockSpec((1, ATTN_WIDTH), full),
            pl.BlockSpec((tm, LANES), pos),
            pl.BlockSpec((tm, LANES), pos),
            pl.BlockSpec((tm, LANES), pos),
        ],
        out_specs=[pl.BlockSpec((tm, ATTN_WIDTH), row)] * 5,
        out_shape=[
            jax.ShapeDtypeStruct((n, ATTN_WIDTH), BF16),
            jax.ShapeDtypeStruct((n, ATTN_WIDTH), F32),
            jax.ShapeDtypeStruct((n, ATTN_WIDTH), F32),
            jax.ShapeDtypeStruct((n, CONV_CH), F32),
            jax.ShapeDtypeStruct((n, ATTN_WIDTH), BF16),
        ],
        compiler_params=_cparams(1),
        name="inproj",
    )(x, g1, w_in_b, bd, qg, kg, ct, s1t, s2t)
    return outs


def _lambda_value(lam_ref, lam_init):
    lp = lam_ref[...]
    t1 = jnp.sum(lp[0:1] * lp[1:2], axis=1, keepdims=True)
    t2 = jnp.sum(lp[2:3] * lp[3:4], axis=1, keepdims=True)
    return jnp.exp(t1) - jnp.exp(t2) + lam_init


def _subln(o, g, lam_init):
    ms = jnp.mean(o * o, axis=-1, keepdims=True)
    return o * lax.rsqrt(ms + EPS) * g * (1.0 - lam_init)


def _pattn_kernel(q_ref, k_ref, v_ref, lam_ref, sgc_ref, o_ref, vt_scr, qs_scr, m_scr, l_scr, acc_scr,
                  *, tq, lam_init):
    i = pl.program_id(1)
    nkv = k_ref.shape[0] // tq
    r2 = 2 * tq

    @pl.when(i == 0)
    def _():
        for c in range(nkv):
            vt_scr[c] = v_ref[c * tq:(c + 1) * tq, :].T.astype(BF16)

    lam = _lambda_value(lam_ref, lam_init)
    lane = lax.broadcasted_iota(jnp.int32, (tq, LANES), 1)
    krow = lax.broadcasted_iota(jnp.int32, (tq, r2), 0)
    qcol = lax.broadcasted_iota(jnp.int32, (tq, r2), 1)
    causal = krow <= jnp.where(qcol >= tq, qcol - tq, qcol)

    for hd in range(N_HEADS):
        qp = q_ref[:, hd * LANES:(hd + 1) * LANES]
        zero = jnp.zeros_like(qp)
        qs_scr[hd] = jnp.concatenate([jnp.where(lane < HEAD_DIM, qp, zero),
                                      jnp.where(lane >= HEAD_DIM, qp, zero)], axis=0)
        m_scr[hd] = jnp.full((1, r2), -jnp.inf, F32)
        l_scr[hd] = jnp.zeros((1, r2), F32)
        acc_scr[hd] = jnp.zeros((V_DIM, r2), F32)

    def step(j, masked):
        off = pl.multiple_of(j * tq, tq)
        for hd in range(N_HEADS):
            sl = slice(hd * LANES, (hd + 1) * LANES)
            kj = k_ref[pl.ds(off, tq), sl]
            st = lax.dot_general(kj, qs_scr[hd], (((1,), (1,)), ((), ())), preferred_element_type=F32)
            if masked:
                st = jnp.where(causal, st, -jnp.inf)
            m_prev = m_scr[hd]
            m_new = jnp.maximum(m_prev, jnp.max(st, axis=0, keepdims=True))
            alpha = jnp.exp(m_prev - m_new)
            p = jnp.exp(st - m_new)
            l_scr[hd] = alpha * l_scr[hd] + jnp.sum(p, axis=0, keepdims=True)
            acc_scr[hd] = alpha * acc_scr[hd] + jnp.dot(vt_scr[j, sl, :], p.astype(BF16),
                                                        preferred_element_type=F32)
            m_scr[hd] = m_new

    def body(j, carry):
        step(j, False)
        return carry

    lax.fori_loop(0, i, body, 0)
    step(i, True)

    sgc = sgc_ref[...]
    for hd in range(N_HEADS):
        ot = acc_scr[hd] / l_scr[hd]
        od = ot[:, 0:tq] - lam * ot[:, tq:r2]
        ms = jnp.mean(od * od, axis=0, keepdims=True)
        on = od * lax.rsqrt(ms + EPS) * sgc * (1.0 - lam_init)
        o_ref[:, hd * LANES:(hd + 1) * LANES] = on.T.astype(BF16)


def _prompt_attention(q, kb, v, lam_p, sgc, batch, seq, tq, lam_init):
    nq = seq // tq
    n = batch * seq
    kern = functools.partial(_pattn_kernel, tq=tq, lam_init=lam_init)
    return pl.pallas_call(
        kern,
        grid=(batch, nq),
        in_specs=[
            pl.BlockSpec((tq, ATTN_WIDTH), lambda b, i: (b * nq + i, 0)),
            pl.BlockSpec((seq, ATTN_WIDTH), lambda b, i: (b, 0)),
            pl.BlockSpec((seq, ATTN_WIDTH), lambda b, i: (b, 0)),
            pl.BlockSpec((4, HEAD_DIM), lambda b, i: (0, 0)),
            pl.BlockSpec((V_DIM, 1), lambda b, i: (0, 0)),
        ],
        out_specs=pl.BlockSpec((tq, ATTN_WIDTH), lambda b, i: (b * nq + i, 0)),
        out_shape=jax.ShapeDtypeStruct((n, ATTN_WIDTH), BF16),
        scratch_shapes=[
            pltpu.VMEM((nq, ATTN_WIDTH, tq), BF16),
            pltpu.VMEM((N_HEADS, 2 * tq, LANES), BF16),
            pltpu.VMEM((N_HEADS, 1, 2 * tq), F32),
            pltpu.VMEM((N_HEADS, 1, 2 * tq), F32),
            pltpu.VMEM((N_HEADS, V_DIM, 2 * tq), F32),
        ],
        compiler_params=_cparams(2),
        name="prompt_attn",
    )(q, kb, v, lam_p, sgc)


def _dattn_kernel(pt_ref, q_ref, qt_ref, kn_ref, vn_ref, lam_ref, sg_ref, *rest, n_pages, page, lam_init):
    k_refs = rest[:n_pages]
    v_refs = rest[n_pages:2 * n_pages]
    o_ref = rest[2 * n_pages]
    del pt_ref
    lam = _lambda_value(lam_ref, lam_init)
    qt = qt_ref[...]
    qcols = [jnp.broadcast_to(qt[:, h:h + 1], (HEAD_DIM, page)) for h in range(N_SUB)]
    s = jnp.concatenate(
        [jnp.concatenate([jnp.sum(k_refs[p][h] * qcols[h], axis=0, keepdims=True) for h in range(N_SUB)], axis=0)
         for p in range(n_pages)], axis=1)
    s_self = jnp.sum(q_ref[...] * kn_ref[...], axis=1, keepdims=True)
    m = jnp.maximum(jnp.max(s, axis=1, keepdims=True), s_self)
    pexp = jnp.exp(s - m)
    pself = jnp.exp(s_self - m)
    inv = 1.0 / (jnp.sum(pexp, axis=1, keepdims=True) + pself)
    sub = lax.broadcasted_iota(jnp.int32, (N_SUB, 1), 0)
    coef = jnp.where((sub & 1) == 0, 1.0, -lam) * inv
    a8 = pexp * coef
    a8 = (a8 + pltpu.roll(a8, N_SUB - 1, 0)).astype(BF16)
    as8 = jnp.broadcast_to(pself * coef, (N_SUB, V_DIM))
    as8 = as8 + pltpu.roll(as8, N_SUB - 1, 0)
    o_self = as8 * vn_ref[...]
    rows = []
    for hd in range(N_HEADS):
        acc = jnp.zeros((N_SUB, V_DIM), F32)
        for p in range(n_pages):
            v_hd = v_refs[p][pl.ds(hd, page, stride=N_HEADS), :].astype(BF16)
            acc = acc + jnp.dot(a8[:, p * page:(p + 1) * page], v_hd, preferred_element_type=F32)
        rows.append(acc[2 * hd:2 * hd + 1] + o_self[2 * hd:2 * hd + 1])
    o = jnp.concatenate(rows, axis=0)
    o_ref[...] = _subln(o, sg_ref[...], lam_init)


def _decode_attention(page_table, q8, qt, kn8, vn8, lam_p, sg, ckt, cvf, layer, lam_init):
    db, n_pages = page_table.shape
    page = ckt.shape[-1]
    kern = functools.partial(_dattn_kernel, n_pages=n_pages, page=page, lam_init=lam_init)
    const2 = lambda b, pt: (0, 0)

    def kmap(p):
        return lambda b, pt: (layer, pt[b, p], 0, 0, 0)

    def vmap_(p):
        return lambda b, pt: (layer, pt[b, p], 0, 0)

    in_specs = [
        pl.BlockSpec((None, N_SUB, HEAD_DIM), lambda b, pt: (b, 0, 0)),
        pl.BlockSpec((None, HEAD_DIM, N_SUB), lambda b, pt: (b, 0, 0)),
        pl.BlockSpec((None, N_SUB, HEAD_DIM), lambda b, pt: (b, 0, 0)),
        pl.BlockSpec((None, N_SUB, V_DIM), lambda b, pt: (b, 0, 0)),
        pl.BlockSpec((4, HEAD_DIM), const2),
        pl.BlockSpec((1, V_DIM), const2),
    ]
    in_specs += [pl.BlockSpec((None, None, N_SUB, HEAD_DIM, page), kmap(p)) for p in range(n_pages)]
    in_specs += [pl.BlockSpec((None, None, page * N_HEADS, V_DIM), vmap_(p)) for p in range(n_pages)]
    grid_spec = pltpu.PrefetchScalarGridSpec(
        num_scalar_prefetch=1,
        grid=(db,),
        in_specs=in_specs,
        out_specs=pl.BlockSpec((None, N_HEADS, V_DIM), lambda b, pt: (b, 0, 0)),
    )
    return pl.pallas_call(
        kern,
        grid_spec=grid_spec,
        out_shape=jax.ShapeDtypeStruct((db, N_HEADS, V_DIM), F32),
        compiler_params=_cparams(1),
        name="decode_attn",
    )(page_table, q8, qt, kn8, vn8, lam_p, sg, *([ckt] * n_pages), *([cvf] * n_pages))


def _conv_post(y, lg_ref, lb_ref):
    mu = jnp.mean(y, axis=-1, keepdims=True)
    yc = y - mu
    var = jnp.mean(yc * yc, axis=-1, keepdims=True)
    yn = yc * lax.rsqrt(var + EPS) * lg_ref[...] + lb_ref[...]
    return yn * jax.nn.sigmoid(yn)


def _mix_tail(x, att, cnv, wo_ref, g2_ref, w1_ref, w2_ref, xo_ref, h_ref, gate_ref):
    y = (x + jnp.dot(att, wo_ref[0:ATTN_WIDTH, :], preferred_element_type=F32)
         + jnp.dot(cnv.astype(BF16), wo_ref[ATTN_WIDTH:, :], preferred_element_type=F32))
    xo_ref[...] = y
    ms = jnp.mean(y * y, axis=-1, keepdims=True)
    h = y * lax.rsqrt(ms + EPS) * g2_ref[...]
    h_hi = h.astype(BF16)
    h_lo = (h - h_hi.astype(F32)).astype(BF16)
    h_ref[...] = h_hi
    r = jnp.dot(h_hi, w1_ref[...], preferred_element_type=F32)
    lg = r + pltpu.roll(r, LANES // 2, 1) + jnp.dot(h_lo, w2_ref[...], preferred_element_type=F32)
    tm = lg.shape[0]
    lane = lax.broadcasted_iota(jnp.int32, (tm, LANES), 1)
    lanef = lane.astype(F32)
    big = float(LANES)
    neg = -jnp.inf
    gmask = lane < N_GROUPS
    gl = jnp.where(gmask, lg, neg)
    gmax = jnp.max(gl, axis=1, keepdims=True)
    gsum = jnp.sum(jnp.where(gmask, jnp.exp(gl - gmax), 0.0), axis=1, keepdims=True)
    g_w = 1.0 / gsum
    g_idx = jnp.min(jnp.where(gl == gmax, lanef, big), axis=1, keepdims=True)
    lo = N_GROUPS + EXP_PER_GROUP * g_idx
    emask = (lanef >= lo) & (lanef < lo + EXP_PER_GROUP)
    ev = jnp.where(emask, lg, neg)
    v1 = jnp.max(ev, axis=1, keepdims=True)
    i1 = jnp.min(jnp.where(ev == v1, lanef, big), axis=1, keepdims=True)
    ev2 = jnp.where(lanef == i1, neg, ev)
    v2 = jnp.max(ev2, axis=1, keepdims=True)
    i2 = jnp.min(jnp.where(ev2 == v2, lanef, big), axis=1, keepdims=True)
    e2 = jnp.exp(v2 - v1)
    den = 1.0 / (1.0 + e2)
    w1 = den * g_w
    w2 = e2 * den * g_w
    gates = jnp.where(lanef == i1, w1, jnp.where(lanef == i2, w2, 0.0))
    gate_ref[...] = pltpu.roll(gates, LANES - N_GROUPS, 1)


def _mix_prompt_kernel(x_ref, att_ref, u_ref, uh_ref, cw_ref, cb_ref, lg_ref, lb_ref, wo_ref, g2_ref,
                       w1_ref, w2_ref, xo_ref, h_ref, gate_ref, ext_scr, *, tm, tiles_per_seq):
    i = pl.program_id(0)
    first = (i % tiles_per_seq) == 0
    ext_scr[0:HALO, :] = jnp.where(first, 0.0, uh_ref[...])
    ext_scr[HALO:HALO + tm, :] = u_ref[...]
    ext_scr[HALO + tm:, :] = jnp.zeros((EXT_TAIL, CONV_CH), F32)
    lead = HALO - CONV_BUF
    acc = jnp.zeros((tm, CONV_CH), F32) + cb_ref[...]
    for r in range(SUBLANES):
        z = None
        for a in range((CONV_W + lead + SUBLANES - 1) // SUBLANES):
            j = SUBLANES * a + r - lead
            if 0 <= j < CONV_W:
                term = cw_ref[j:j + 1, :] * ext_scr[SUBLANES * a:SUBLANES * a + tm + SUBLANES, :]
                z = term if z is None else z + term
        acc = acc + z[r:r + tm]
    cnv = _conv_post(acc, lg_ref, lb_ref)
    _mix_tail(x_ref[...], att_ref[...], cnv, wo_ref, g2_ref, w1_ref, w2_ref, xo_ref, h_ref, gate_ref)


def _mix_decode_kernel(x_ref, att_ref, u_ref, st_ref, cw_ref, cb_ref, lg_ref, lb_ref, wo_ref, g2_ref,
                       w1_ref, w2_ref, xo_ref, h_ref, gate_ref, ns_ref):
    u = u_ref[...]
    acc = cb_ref[...] + cw_ref[CONV_BUF:CONV_BUF + 1, :] * u
    for j in range(CONV_BUF):
        acc = acc + cw_ref[j:j + 1, :] * st_ref[j]
    for j in range(CONV_BUF - 1):
        ns_ref[j] = st_ref[j + 1]
    ns_ref[CONV_BUF - 1] = u
    cnv = _conv_post(acc, lg_ref, lb_ref)
    _mix_tail(x_ref[...], att_ref[...], cnv, wo_ref, g2_ref, w1_ref, w2_ref, xo_ref, h_ref, gate_ref)


def _mix_common_specs(tm):
    row = lambda i: (i, 0)
    full = lambda i: (0, 0)
    tail_in = [
        pl.BlockSpec((HALO, CONV_CH), full),
        pl.BlockSpec((1, CONV_CH), full),
        pl.BlockSpec((1, CONV_CH), full),
        pl.BlockSpec((1, CONV_CH), full),
        pl.BlockSpec((D_MODEL, D_MODEL), full),
        pl.BlockSpec((1, D_MODEL), full),
        pl.BlockSpec((D_MODEL, LANES), full),
        pl.BlockSpec((D_MODEL, LANES), full),
    ]
    out_specs = [pl.BlockSpec((tm, D_MODEL), row), pl.BlockSpec((tm, D_MODEL), row),
                 pl.BlockSpec((tm, LANES), row)]
    return row, tail_in, out_specs


def _mix_out_shape(n):
    return [jax.ShapeDtypeStruct((n, D_MODEL), F32), jax.ShapeDtypeStruct((n, D_MODEL), BF16),
            jax.ShapeDtypeStruct((n, LANES), F32)]


def _mix_prompt(x, att, u, cwp, cb, lg, lb, wo_b, g2, w1, w2, seq, tm):
    n = x.shape[0]
    row, tail_in, out_specs = _mix_common_specs(tm)
    halo_map = lambda i: (jnp.maximum(i * (tm // HALO) - 1, 0), 0)
    kern = functools.partial(_mix_prompt_kernel, tm=tm, tiles_per_seq=seq // tm)
    return pl.pallas_call(
        kern,
        grid=(n // tm,),
        in_specs=[pl.BlockSpec((tm, D_MODEL), row), pl.BlockSpec((tm, ATTN_WIDTH), row),
                  pl.BlockSpec((tm, CONV_CH), row), pl.BlockSpec((HALO, CONV_CH), halo_map)] + tail_in,
        out_specs=out_specs,
        out_shape=_mix_out_shape(n),
        scratch_shapes=[pltpu.VMEM((tm + HALO + EXT_TAIL, CONV_CH), F32)],
        compiler_params=_cparams(1),
        name="mix_prompt",
    )(x, att, u, u, cwp, cb, lg, lb, wo_b, g2, w1, w2)


def _mix_decode(x, att, u, state_t, layer, cwp, cb, lg, lb, wo_b, g2, w1, w2, tm):
    n = x.shape[0]
    row, tail_in, out_specs = _mix_common_specs(tm)
    return pl.pallas_call(
        _mix_decode_kernel,
        grid=(n // tm,),
        in_specs=[pl.BlockSpec((tm, D_MODEL), row), pl.BlockSpec((tm, ATTN_WIDTH), row),
                  pl.BlockSpec((tm, CONV_CH), row),
                  pl.BlockSpec((None, CONV_BUF, tm, CONV_CH), lambda i: (layer, 0, i, 0))] + tail_in,
        out_specs=out_specs + [pl.BlockSpec((CONV_BUF, tm, CONV_CH), lambda i: (0, i, 0))],
        out_shape=_mix_out_shape(n) + [jax.ShapeDtypeStruct((CONV_BUF, n, CONV_CH), F32)],
        compiler_params=_cparams(1),
        name="mix_decode",
    )(x, att, u, state_t, cwp, cb, lg, lb, wo_b, g2, w1, w2)


def _moe_kernel(x_ref, h_ref, gate_ref, wg_ref, wu_ref, wd_ref, o_ref):
    e = pl.program_id(1)

    @pl.when(e == 0)
    def _():
        o_ref[...] = x_ref[...]

    h = h_ref[...]
    hg = jnp.dot(h, wg_ref[...], preferred_element_type=F32)
    hu = jnp.dot(h, wu_ref[...], preferred_element_type=F32)
    gates = gate_ref[...]
    lane = lax.broadcasted_iota(jnp.int32, gates.shape, 1)
    ge = jnp.sum(jnp.where(lane == e, gates, 0.0), axis=1, keepdims=True)
    act = (hg * jax.nn.sigmoid(hg)) * hu * ge
    o_ref[...] += jnp.dot(act.astype(BF16), wd_ref[...], preferred_element_type=F32)


def _moe(x, h, gates, wg_b, wu_b, wd_b, tm):
    n = x.shape[0]
    row = lambda i, e: (i, 0)
    return pl.pallas_call(
        _moe_kernel,
        grid=(n // tm, N_EXPERTS),
        in_specs=[
            pl.BlockSpec((tm, D_MODEL), row),
            pl.BlockSpec((tm, D_MODEL), row),
            pl.BlockSpec((tm, LANES), row),
            pl.BlockSpec((None, D_MODEL, D_EXPERT), lambda i, e: (e, 0, 0)),
            pl.BlockSpec((None, D_MODEL, D_EXPERT), lambda i, e: (e, 0, 0)),
            pl.BlockSpec((None, D_EXPERT, D_MODEL), lambda i, e: (e, 0, 0)),
        ],
        out_specs=pl.BlockSpec((tm, D_MODEL), row),
        out_shape=jax.ShapeDtypeStruct((n, D_MODEL), F32),
        compiler_params=_cparams(2),
        name="moe",
    )(x, h, gates, wg_b, wu_b, wd_b)


def _rope_tables(pos):
    half = ROT_DIM // 2
    inv = jnp.power(ROPE_THETA, -jnp.arange(0, ROT_DIM, 2, dtype=F32) / ROT_DIM)
    ang = pos.astype(F32)[:, None] * inv[None, :]
    cos, sin = jnp.cos(ang), jnp.sin(ang)
    n = pos.shape[0]
    pad = jnp.zeros((n, HEAD_DIM - ROT_DIM), F32)
    zer = jnp.zeros((n, half), F32)
    c = jnp.concatenate([cos, cos, pad + 1.0], axis=1)
    s1 = jnp.concatenate([-sin, zer, pad], axis=1)
    s2 = jnp.concatenate([zer, sin, pad], axis=1)
    rep = LANES // HEAD_DIM
    return jnp.tile(c, (1, rep)), jnp.tile(s1, (1, rep)), jnp.tile(s2, (1, rep))


def kernel(x_prompt, x_sample, cache_k, cache_v, state_conv, page_table, norm1_g, w_in, q_norm_g, k_norm_g,
           lam_q1, lam_k1, lam_q2, lam_k2, subln_g, conv_w, conv_b, conv_ln_g, conv_ln_b, w_out, norm2_g,
           w_router_group, w_router_expert, w_gate, w_up, w_down):
    batch, seq, d = x_prompt.shape
    db = x_sample.shape[0]
    depth = w_in.shape[0]
    n_pages, page = page_table.shape[1], cache_k.shape[2]
    past = n_pages * page
    n = batch * seq

    xp = x_prompt.reshape(n, d)
    xs = x_sample.reshape(db, d)
    tabs_p = _rope_tables(jnp.arange(seq, dtype=jnp.int32))
    tabs_s = _rope_tables(jnp.full((db,), past, dtype=jnp.int32))
    gi = jnp.arange(ATTN_WIDTH, dtype=jnp.int32) // HEAD_DIM
    bd = (gi[:, None] == gi[None, :]).astype(BF16)
    ckt = jnp.transpose(cache_k, (0, 1, 3, 4, 2))
    cvf = cache_v.reshape(depth, cache_v.shape[1], page * N_HEADS, V_DIM)
    state_t = jnp.transpose(state_conv, (0, 2, 1, 3))

    kp_l, vp_l, cp_l, ks_l, vs_l, cs_l = [], [], [], [], [], []
    for l in range(depth):
        lam_init = 0.8 - 0.6 * math.exp(-0.3 * l)
        g1 = norm1_g[l][None, :]
        w_in_b = w_in[l].astype(BF16)
        qg = jnp.tile(q_norm_g[l], N_SUB)[None, :]
        kg = jnp.tile(k_norm_g[l], N_SUB)[None, :]
        lam_p = jnp.stack([lam_q1[l], lam_k1[l], lam_q2[l], lam_k2[l]])
        sg = subln_g[l][None, :]
        cwp = jnp.pad(conv_w[l], ((0, HALO - CONV_W), (0, 0)))
        cb, lg, lb = conv_b[l][None, :], conv_ln_g[l][None, :], conv_ln_b[l][None, :]
        wo_b = w_out[l].astype(BF16)
        g2 = norm2_g[l][None, :]
        wr = jnp.concatenate([w_router_group[l], w_router_expert[l]], axis=1)
        wr_hi = wr.astype(BF16)
        wr_lo = (wr - wr_hi.astype(F32)).astype(BF16)
        nr = wr.shape[1]
        w1 = jnp.zeros((d, LANES), BF16).at[:, :nr].set(wr_hi).at[:, LANES // 2:LANES // 2 + nr].set(wr_lo)
        w2 = jnp.zeros((d, LANES), BF16).at[:, :nr].set(wr_hi)
        wg_b, wu_b, wd_b = w_gate[l].astype(BF16), w_up[l].astype(BF16), w_down[l].astype(BF16)

        q, k, v, u, kb = _inproj(xp, g1, w_in_b, bd, qg, kg, *tabs_p, tm=512)
        att = _prompt_attention(q, kb, v, lam_p, sg.reshape(V_DIM, 1), batch, seq, 512, lam_init)
        xp, h2, gates = _mix_prompt(xp, att, u, cwp, cb, lg, lb, wo_b, g2, w1, w2, seq, 256)
        xp = _moe(xp, h2, gates, wg_b, wu_b, wd_b, 1024)
        kp_l.append(k.reshape(batch, seq, N_SUB, HEAD_DIM))
        vp_l.append(v.reshape(batch, seq, N_HEADS, V_DIM))
        cp_l.append(u.reshape(batch, seq, CONV_CH)[:, seq - CONV_BUF:])

        q, k, v, u, _ = _inproj(xs, g1, w_in_b, bd, qg, kg, *tabs_s, tm=db)
        q8 = q.astype(F32).reshape(db, N_SUB, HEAD_DIM)
        vn8 = jnp.repeat(v.reshape(db, N_HEADS, V_DIM), 2, axis=1)
        att = _decode_attention(page_table, q8, jnp.transpose(q8, (0, 2, 1)), k.reshape(db, N_SUB, HEAD_DIM),
                                vn8, lam_p, sg, ckt, cvf, l, lam_init)
        att = att.reshape(db, ATTN_WIDTH).astype(BF16)
        xs, h2, gates, ns = _mix_decode(xs, att, u, state_t, l, cwp, cb, lg, lb, wo_b, g2, w1, w2, 64)
        xs = _moe(xs, h2, gates, wg_b, wu_b, wd_b, db)
        ks_l.append(k.reshape(db, 1, N_SUB, HEAD_DIM))
        vs_l.append(v.reshape(db, 1, N_HEADS, V_DIM))
        cs_l.append(jnp.transpose(ns, (1, 0, 2)))

    return (xp.reshape(batch, seq, d), xs.reshape(db, 1, d), jnp.stack(kp_l), jnp.stack(vp_l), jnp.stack(cp_l),
            jnp.stack(ks_l), jnp.stack(vs_l), jnp.stack(cs_l))
```

```python
import functools
import math

import jax
import jax.numpy as jnp
from jax import lax
from jax.experimental import pallas as pl
from jax.experimental.pallas import tpu as pltpu

F32 = jnp.float32
BF16 = jnp.bfloat16

D_MODEL = 1024
HEAD_DIM = 64
N_SUB = 8
N_HEADS = 4
V_DIM = 128
ATTN_WIDTH = 512
CONV_CH = 512
ROT_DIM = 16
ROPE_THETA = 500000.0
CONV_W = 31
CONV_BUF = CONV_W - 1
N_GROUPS = 4
EXP_PER_GROUP = 4
N_EXPERTS = 16
D_EXPERT = 512
EPS = 1e-6
D_IN = 3 * ATTN_WIDTH + 2 * CONV_CH

LANES = 128
SUBLANES = 8
HALO = 32
EXT_TAIL = 16
GROUP_LANE = N_EXPERTS
VMEM_LIMIT = 56 * 1024 * 1024


def _cparams(n_axes):
    return pltpu.CompilerParams(dimension_semantics=("arbitrary",) * n_axes,
                                vmem_limit_bytes=VMEM_LIMIT)


def _inproj_kernel(x_ref, g1_ref, w_ref, bd_ref, qg_ref, kg_ref, c_ref, s1_ref, s2_ref,
                   q_ref, k_ref, v_ref, u_ref, kb_ref):
    x = x_ref[...]
    ms = jnp.mean(x * x, axis=-1, keepdims=True)
    h = (x * lax.rsqrt(ms + EPS) * g1_ref[...]).astype(BF16)
    z = jnp.dot(h, w_ref[...], preferred_element_type=F32)
    c, s1, s2 = c_ref[...], s1_ref[...], s2_ref[...]
    bd = bd_ref[...]

    def head_norm_rope(t, g):
        hms = jnp.dot((t * t).astype(BF16), bd, preferred_element_type=F32) * (1.0 / HEAD_DIM)
        tn = t * lax.rsqrt(hms + EPS) * g
        outs = []
        for j in range(ATTN_WIDTH // LANES):
            blk = tn[:, j * LANES:(j + 1) * LANES]
            up = pltpu.roll(blk, LANES - ROT_DIM // 2, 1)
            dn = pltpu.roll(blk, ROT_DIM // 2, 1)
            outs.append(blk * c + up * s1 + dn * s2)
        return outs

    qs = head_norm_rope(z[:, 0:ATTN_WIDTH], qg_ref[...])
    ks = head_norm_rope(z[:, ATTN_WIDTH:2 * ATTN_WIDTH], kg_ref[...])
    for j in range(ATTN_WIDTH // LANES):
        sl = slice(j * LANES, (j + 1) * LANES)
        q_ref[:, sl] = (qs[j] * (HEAD_DIM ** -0.5)).astype(BF16)
        k_ref[:, sl] = ks[j]
        kb_ref[:, sl] = ks[j].astype(BF16)
    v_ref[...] = z[:, 2 * ATTN_WIDTH:3 * ATTN_WIDTH]
    a = z[:, 3 * ATTN_WIDTH:3 * ATTN_WIDTH + CONV_CH]
    gt = z[:, 3 * ATTN_WIDTH + CONV_CH:]
    u_ref[...] = a * jax.nn.sigmoid(gt)


def _inproj(x, g1, w_in_b, bd, qg, kg, ct, s1t, s2t, tm):
    n = x.shape[0]
    npos = ct.shape[0] // tm
    row = lambda i: (i, 0)
    full = lambda i: (0, 0)
    pos = lambda i: (i % npos, 0)
    outs = pl.pallas_call(
        _inproj_kernel,
        grid=(n // tm,),
        in_specs=[
            pl.BlockSpec((tm, D_MODEL), row),
            pl.BlockSpec((1, D_MODEL), full),
            pl.BlockSpec((D_MODEL, D_IN), full),
            pl.BlockSpec((ATTN_WIDTH, ATTN_WIDTH), full),
            pl.BlockSpec((1, ATTN_WIDTH), full),
            pl.BlockSpec((1, ATTN_WIDTH), full),
            pl.BlockSpec((tm, LANES), pos),
            pl.BlockSpec((tm, LANES), pos),
            pl.BlockSpec((tm, LANES), pos),
        ],
        out_specs=[pl.BlockSpec((tm, ATTN_WIDTH), row)] * 5,
        out_shape=[
            jax.ShapeDtypeStruct((n, ATTN_WIDTH), BF16),
            jax.ShapeDtypeStruct((n, ATTN_WIDTH), F32),
            jax.ShapeDtypeStruct((n, ATTN_WIDTH), F32),
            jax.ShapeDtypeStruct((n, CONV_CH), F32),
            jax.ShapeDtypeStruct((n, ATTN_WIDTH), BF16),
        ],
        compiler_params=_cparams(1),
        name="inproj",
    )(x, g1, w_in_b, bd, qg, kg, ct, s1t, s2t)
    return outs


def _lambda_value(lam_ref, lam_init):
    lp = lam_ref[...]
    t1 = jnp.sum(lp[0:1] * lp[1:2], axis=1, keepdims=True)
    t2 = jnp.sum(lp[2:3] * lp[3:4], axis=1, keepdims=True)
    return jnp.exp(t1) - jnp.exp(t2) + lam_init


def _subln(o, g, lam_init):
    ms = jnp.mean(o * o, axis=-1, keepdims=True)
    return o * lax.rsqrt(ms + EPS) * g * (1.0 - lam_init)


def _pattn_kernel(q_ref, k_ref, v_ref, lam_ref, sgc_ref, o_ref, vt_scr, qs_scr, m_scr, l_scr, acc_scr,
                  *, tq, lam_init):
    i = pl.program_id(1)
    nkv = k_ref.shape[0] // tq
    r2 = 2 * tq

    @pl.when(i == 0)
    def _():
        for c in range(nkv):
            vt_scr[c] = v_ref[c * tq:(c + 1) * tq, :].T.astype(BF16)

    lam = _lambda_value(lam_ref, lam_init)
    lane = lax.broadcasted_iota(jnp.int32, (tq, LANES), 1)
    krow = lax.broadcasted_iota(jnp.int32, (tq, r2), 0)
    qcol = lax.broadcasted_iota(jnp.int32, (tq, r2), 1)
    causal = krow <= jnp.where(qcol >= tq, qcol - tq, qcol)

    for hd in range(N_HEADS):
        qp = q_ref[:, hd * LANES:(hd + 1) * LANES]
        zero = jnp.zeros_like(qp)
        qs_scr[hd] = jnp.concatenate([jnp.where(lane < HEAD_DIM, qp, zero),
                                      jnp.where(lane >= HEAD_DIM, qp, zero)], axis=0)
        m_scr[hd] = jnp.full((1, r2), -jnp.inf, F32)
        l_scr[hd] = jnp.zeros((1, r2), F32)
        acc_scr[hd] = jnp.zeros((V_DIM, r2), F32)

    def step(j, masked):
        off = pl.multiple_of(j * tq, tq)
        for hd in range(N_HEADS):
            sl = slice(hd * LANES, (hd + 1) * LANES)
            kj = k_ref[pl.ds(off, tq), sl]
            st = lax.dot_general(kj, qs_scr[hd], (((1,), (1,)), ((), ())), preferred_element_type=F32)
            if masked:
                st = jnp.where(causal, st, -jnp.inf)
            m_prev = m_scr[hd]
            m_new = jnp.maximum(m_prev, jnp.max(st, axis=0, keepdims=True))
            alpha = jnp.exp(m_prev - m_new)
            p = jnp.exp(st - m_new)
            l_scr[hd] = alpha * l_scr[hd] + jnp.sum(p, axis=0, keepdims=True)
            acc_scr[hd] = alpha * acc_scr[hd] + jnp.dot(vt_scr[j, sl, :], p.astype(BF16),
                                                        preferred_element_type=F32)
            m_scr[hd] = m_new

    def body(j, carry):
        step(j, False)
        return carry

    lax.fori_loop(0, i, body, 0)
    step(i, True)

    sgc = sgc_ref[...]
    for hd in range(N_HEADS):
        ot = acc_scr[hd] / l_scr[hd]
        od = ot[:, 0:tq] - lam * ot[:, tq:r2]
        ms = jnp.mean(od * od, axis=0, keepdims=True)
        on = od * lax.rsqrt(ms + EPS) * sgc * (1.0 - lam_init)
        o_ref[:, hd * LANES:(hd + 1) * LANES] = on.T.astype(BF16)


def _prompt_attention(q, kb, v, lam_p, sgc, batch, seq, tq, lam_init):
    nq = seq // tq
    n = batch * seq
    kern = functools.partial(_pattn_kernel, tq=tq, lam_init=lam_init)
    return pl.pallas_call(
        kern,
        grid=(batch, nq),
        in_specs=[
            pl.BlockSpec((tq, ATTN_WIDTH), lambda b, i: (b * nq + i, 0)),
            pl.BlockSpec((seq, ATTN_WIDTH), lambda b, i: (b, 0)),
            pl.BlockSpec((seq, ATTN_WIDTH), lambda b, i: (b, 0)),
            pl.BlockSpec((4, HEAD_DIM), lambda b, i: (0, 0)),
            pl.BlockSpec((V_DIM, 1), lambda b, i: (0, 0)),
        ],
        out_specs=pl.BlockSpec((tq, ATTN_WIDTH), lambda b, i: (b * nq + i, 0)),
        out_shape=jax.ShapeDtypeStruct((n, ATTN_WIDTH), BF16),
        scratch_shapes=[
            pltpu.VMEM((nq, ATTN_WIDTH, tq), BF16),
            pltpu.VMEM((N_HEADS, 2 * tq, LANES), BF16),
            pltpu.VMEM((N_HEADS, 1, 2 * tq), F32),
            pltpu.VMEM((N_HEADS, 1, 2 * tq), F32),
            pltpu.VMEM((N_HEADS, V_DIM, 2 * tq), F32),
        ],
        compiler_params=_cparams(2),
        name="prompt_attn",
    )(q, kb, v, lam_p, sgc)


def _dattn_kernel(pt_ref, q_ref, qt_ref, kn_ref, vn_ref, lam_ref, sg_ref, *rest, n_pages, page, lam_init):
    k_refs = rest[:n_pages]
    v_refs = rest[n_pages:2 * n_pages]
    o_ref = rest[2 * n_pages]
    del pt_ref
    lam = _lambda_value(lam_ref, lam_init)
    qt = qt_ref[...]
    qcols = [jnp.broadcast_to(qt[:, h:h + 1], (HEAD_DIM, page)) for h in range(N_SUB)]
    s = jnp.concatenate(
        [jnp.concatenate([jnp.sum(k_refs[p][h] * qcols[h], axis=0, keepdims=True) for h in range(N_SUB)], axis=0)
         for p in range(n_pages)], axis=1)
    s_self = jnp.sum(q_ref[...] * kn_ref[...], axis=1, keepdims=True)
    m = jnp.maximum(jnp.max(s, axis=1, keepdims=True), s_self)
    pexp = jnp.exp(s - m)
    pself = jnp.exp(s_self - m)
    inv = 1.0 / (jnp.sum(pexp, axis=1, keepdims=True) + pself)
    sub = lax.broadcasted_iota(jnp.int32, (N_SUB, 1), 0)
    coef = jnp.where((sub & 1) == 0, 1.0, -lam) * inv
    a8 = pexp * coef
    a8 = (a8 + pltpu.roll(a8, N_SUB - 1, 0)).astype(BF16)
    as8 = jnp.broadcast_to(pself * coef, (N_SUB, V_DIM))
    as8 = as8 + pltpu.roll(as8, N_SUB - 1, 0)
    o_self = as8 * vn_ref[...]
    rows = []
    for hd in range(N_HEADS):
        acc = jnp.zeros((N_SUB, V_DIM), F32)
        for p in range(n_pages):
            v_hd = v_refs[p][pl.ds(hd, page, stride=N_HEADS), :].astype(BF16)
            acc = acc + jnp.dot(a8[:, p * page:(p + 1) * page], v_hd, preferred_element_type=F32)
        rows.append(acc[2 * hd:2 * hd + 1] + o_self[2 * hd:2 * hd + 1])
    o = jnp.concatenate(rows, axis=0)
    o_ref[...] = _subln(o, sg_ref[...], lam_init)


def _decode_attention(page_table, q8, qt, kn8, vn8, lam_p, sg, ckt, cvf, layer, lam_init):
    db, n_pages = page_table.shape
    page = ckt.shape[-1]
    kern = functools.partial(_dattn_kernel, n_pages=n_pages, page=page, lam_init=lam_init)
    const2 = lambda b, pt: (0, 0)

    def kmap(p):
        return lambda b, pt: (layer, pt[b, p], 0, 0, 0)

    def vmap_(p):
        return lambda b, pt: (layer, pt[b, p], 0, 0)

    in_specs = [
        pl.BlockSpec((None, N_SUB, HEAD_DIM), lambda b, pt: (b, 0, 0)),
        pl.BlockSpec((None, HEAD_DIM, N_SUB), lambda b, pt: (b, 0, 0)),
        pl.BlockSpec((None, N_SUB, HEAD_DIM), lambda b, pt: (b, 0, 0)),
        pl.BlockSpec((None, N_SUB, V_DIM), lambda b, pt: (b, 0, 0)),
        pl.BlockSpec((4, HEAD_DIM), const2),
        pl.BlockSpec((1, V_DIM), const2),
    ]
    in_specs += [pl.BlockSpec((None, None, N_SUB, HEAD_DIM, page), kmap(p)) for p in range(n_pages)]
    in_specs += [pl.BlockSpec((None, None, page * N_HEADS, V_DIM), vmap_(p)) for p in range(n_pages)]
    grid_spec = pltpu.PrefetchScalarGridSpec(
        num_scalar_prefetch=1,
        grid=(db,),
        in_specs=in_specs,
        out_specs=pl.BlockSpec((None, N_HEADS, V_DIM), lambda b, pt: (b, 0, 0)),
    )
    return pl.pallas_call(
        kern,
        grid_spec=grid_spec,
        out_shape=jax.ShapeDtypeStruct((db, N_HEADS, V_DIM), F32),
        compiler_params=_cparams(1),
        name="decode_attn",
    )(page_table, q8, qt, kn8, vn8, lam_p, sg, *([ckt] * n_pages), *([cvf] * n_pages))


def _conv_post(y, lg_ref, lb_ref):
    mu = jnp.mean(y, axis=-1, keepdims=True)
    yc = y - mu
    var = jnp.mean(yc * yc, axis=-1, keepdims=True)
    yn = yc * lax.rsqrt(var + EPS) * lg_ref[...] + lb_ref[...]
    return yn * jax.nn.sigmoid(yn)


def _mix_tail(x, att, cnv, wo_ref, g2_ref, w1_ref, w2_ref, xo_ref, h_ref, gate_ref, grow_ref=None):
    y = (x + jnp.dot(att, wo_ref[0:ATTN_WIDTH, :], preferred_element_type=F32)
         + jnp.dot(cnv.astype(BF16), wo_ref[ATTN_WIDTH:, :], preferred_element_type=F32))
    xo_ref[...] = y
    ms = jnp.mean(y * y, axis=-1, keepdims=True)
    h = y * lax.rsqrt(ms + EPS) * g2_ref[...]
    h_hi = h.astype(BF16)
    h_lo = (h - h_hi.astype(F32)).astype(BF16)
    h_ref[...] = h_hi
    r = jnp.dot(h_hi, w1_ref[...], preferred_element_type=F32)
    lg = r + pltpu.roll(r, LANES // 2, 1) + jnp.dot(h_lo, w2_ref[...], preferred_element_type=F32)
    tm = lg.shape[0]
    lane = lax.broadcasted_iota(jnp.int32, (tm, LANES), 1)
    lanef = lane.astype(F32)
    big = float(LANES)
    neg = -jnp.inf
    gmask = lane < N_GROUPS
    gl = jnp.where(gmask, lg, neg)
    gmax = jnp.max(gl, axis=1, keepdims=True)
    gsum = jnp.sum(jnp.where(gmask, jnp.exp(gl - gmax), 0.0), axis=1, keepdims=True)
    g_w = 1.0 / gsum
    g_idx = jnp.min(jnp.where(gl == gmax, lanef, big), axis=1, keepdims=True)
    lo = N_GROUPS + EXP_PER_GROUP * g_idx
    emask = (lanef >= lo) & (lanef < lo + EXP_PER_GROUP)
    ev = jnp.where(emask, lg, neg)
    v1 = jnp.max(ev, axis=1, keepdims=True)
    i1 = jnp.min(jnp.where(ev == v1, lanef, big), axis=1, keepdims=True)
    ev2 = jnp.where(lanef == i1, neg, ev)
    v2 = jnp.max(ev2, axis=1, keepdims=True)
    i2 = jnp.min(jnp.where(ev2 == v2, lanef, big), axis=1, keepdims=True)
    e2 = jnp.exp(v2 - v1)
    den = 1.0 / (1.0 + e2)
    w1 = den * g_w
    w2 = e2 * den * g_w
    gates = jnp.where(lanef == i1, w1, jnp.where(lanef == i2, w2, 0.0))
    gates = pltpu.roll(gates, LANES - N_GROUPS, 1)
    gate_ref[...] = jnp.where(lane == GROUP_LANE, g_idx, gates)
    if grow_ref is not None:
        grow_ref[...] = jnp.transpose(jnp.broadcast_to(g_idx, (tm, LANES)))[0:SUBLANES, :]


def _mix_prompt_kernel(x_ref, att_ref, u_ref, uh_ref, cw_ref, cb_ref, lg_ref, lb_ref, wo_ref, g2_ref,
                       w1_ref, w2_ref, xo_ref, h_ref, gate_ref, grow_ref, ext_scr, *, tm, tiles_per_seq):
    i = pl.program_id(0)
    first = (i % tiles_per_seq) == 0
    ext_scr[0:HALO, :] = jnp.where(first, 0.0, uh_ref[...])
    ext_scr[HALO:HALO + tm, :] = u_ref[...]
    ext_scr[HALO + tm:, :] = jnp.zeros((EXT_TAIL, CONV_CH), F32)
    lead = HALO - CONV_BUF
    acc = jnp.zeros((tm, CONV_CH), F32) + cb_ref[...]
    for r in range(SUBLANES):
        z = None
        for a in range((CONV_W + lead + SUBLANES - 1) // SUBLANES):
            j = SUBLANES * a + r - lead
            if 0 <= j < CONV_W:
                term = cw_ref[j:j + 1, :] * ext_scr[SUBLANES * a:SUBLANES * a + tm + SUBLANES, :]
                z = term if z is None else z + term
        acc = acc + z[r:r + tm]
    cnv = _conv_post(acc, lg_ref, lb_ref)
    _mix_tail(x_ref[...], att_ref[...], cnv, wo_ref, g2_ref, w1_ref, w2_ref, xo_ref, h_ref, gate_ref, grow_ref)


def _mix_decode_kernel(x_ref, att_ref, u_ref, st_ref, cw_ref, cb_ref, lg_ref, lb_ref, wo_ref, g2_ref,
                       w1_ref, w2_ref, xo_ref, h_ref, gate_ref, ns_ref):
    u = u_ref[...]
    acc = cb_ref[...] + cw_ref[CONV_BUF:CONV_BUF + 1, :] * u
    for j in range(CONV_BUF):
        acc = acc + cw_ref[j:j + 1, :] * st_ref[j]
    for j in range(CONV_BUF - 1):
        ns_ref[j] = st_ref[j + 1]
    ns_ref[CONV_BUF - 1] = u
    cnv = _conv_post(acc, lg_ref, lb_ref)
    _mix_tail(x_ref[...], att_ref[...], cnv, wo_ref, g2_ref, w1_ref, w2_ref, xo_ref, h_ref, gate_ref)


def _mix_common_specs(tm):
    row = lambda i: (i, 0)
    full = lambda i: (0, 0)
    tail_in = [
        pl.BlockSpec((HALO, CONV_CH), full),
        pl.BlockSpec((1, CONV_CH), full),
        pl.BlockSpec((1, CONV_CH), full),
        pl.BlockSpec((1, CONV_CH), full),
        pl.BlockSpec((D_MODEL, D_MODEL), full),
        pl.BlockSpec((1, D_MODEL), full),
        pl.BlockSpec((D_MODEL, LANES), full),
        pl.BlockSpec((D_MODEL, LANES), full),
    ]
    out_specs = [pl.BlockSpec((tm, D_MODEL), row), pl.BlockSpec((tm, D_MODEL), row),
                 pl.BlockSpec((tm, LANES), row)]
    return row, tail_in, out_specs


def _mix_out_shape(n):
    return [jax.ShapeDtypeStruct((n, D_MODEL), F32), jax.ShapeDtypeStruct((n, D_MODEL), BF16),
            jax.ShapeDtypeStruct((n, LANES), F32)]


def _mix_prompt(x, att, u, cwp, cb, lg, lb, wo_b, g2, w1, w2, seq, tm):
    n = x.shape[0]
    row, tail_in, out_specs = _mix_common_specs(tm)
    halo_map = lambda i: (jnp.maximum(i * (tm // HALO) - 1, 0), 0)
    kern = functools.partial(_mix_prompt_kernel, tm=tm, tiles_per_seq=seq // tm)
    return pl.pallas_call(
        kern,
        grid=(n // tm,),
        in_specs=[pl.BlockSpec((tm, D_MODEL), row), pl.BlockSpec((tm, ATTN_WIDTH), row),
                  pl.BlockSpec((tm, CONV_CH), row), pl.BlockSpec((HALO, CONV_CH), halo_map)] + tail_in,
        out_specs=out_specs + [pl.BlockSpec((SUBLANES, tm), lambda i: (0, i))],
        out_shape=_mix_out_shape(n) + [jax.ShapeDtypeStruct((SUBLANES, n), F32)],
        scratch_shapes=[pltpu.VMEM((tm + HALO + EXT_TAIL, CONV_CH), F32)],
        compiler_params=_cparams(1),
        name="mix_prompt",
    )(x, att, u, u, cwp, cb, lg, lb, wo_b, g2, w1, w2)


def _mix_decode(x, att, u, state_t, layer, cwp, cb, lg, lb, wo_b, g2, w1, w2, tm):
    n = x.shape[0]
    row, tail_in, out_specs = _mix_common_specs(tm)
    return pl.pallas_call(
        _mix_decode_kernel,
        grid=(n // tm,),
        in_specs=[pl.BlockSpec((tm, D_MODEL), row), pl.BlockSpec((tm, ATTN_WIDTH), row),
                  pl.BlockSpec((tm, CONV_CH), row),
                  pl.BlockSpec((None, CONV_BUF, tm, CONV_CH), lambda i: (layer, 0, i, 0))] + tail_in,
        out_specs=out_specs + [pl.BlockSpec((CONV_BUF, tm, CONV_CH), lambda i: (0, i, 0))],
        out_shape=_mix_out_shape(n) + [jax.ShapeDtypeStruct((CONV_BUF, n, CONV_CH), F32)],
        compiler_params=_cparams(1),
        name="mix_decode",
    )(x, att, u, state_t, cwp, cb, lg, lb, wo_b, g2, w1, w2)


def _moe_kernel(x_ref, h_ref, gate_ref, wg_ref, wu_ref, wd_ref, o_ref):
    e = pl.program_id(1)

    @pl.when(e == 0)
    def _():
        o_ref[...] = x_ref[...]

    h = h_ref[...]
    hg = jnp.dot(h, wg_ref[...], preferred_element_type=F32)
    hu = jnp.dot(h, wu_ref[...], preferred_element_type=F32)
    gates = gate_ref[...]
    lane = lax.broadcasted_iota(jnp.int32, gates.shape, 1)
    ge = jnp.sum(jnp.where(lane == e, gates, 0.0), axis=1, keepdims=True)
    act = (hg * jax.nn.sigmoid(hg)) * hu * ge
    o_ref[...] += jnp.dot(act.astype(BF16), wd_ref[...], preferred_element_type=F32)


def _moe(x, h, gates, wg_b, wu_b, wd_b, tm):
    n = x.shape[0]
    row = lambda i, e: (i, 0)
    return pl.pallas_call(
        _moe_kernel,
        grid=(n // tm, N_EXPERTS),
        in_specs=[
            pl.BlockSpec((tm, D_MODEL), row),
            pl.BlockSpec((tm, D_MODEL), row),
            pl.BlockSpec((tm, LANES), row),
            pl.BlockSpec((None, D_MODEL, D_EXPERT), lambda i, e: (e, 0, 0)),
            pl.BlockSpec((None, D_MODEL, D_EXPERT), lambda i, e: (e, 0, 0)),
            pl.BlockSpec((None, D_EXPERT, D_MODEL), lambda i, e: (e, 0, 0)),
        ],
        out_specs=pl.BlockSpec((tm, D_MODEL), row),
        out_shape=jax.ShapeDtypeStruct((n, D_MODEL), F32),
        compiler_params=_cparams(2),
        name="moe",
    )(x, h, gates, wg_b, wu_b, wd_b)


def _moe_sparse_kernel(x_ref, h_ref, gate_ref, grow_ref, wg_ref, wu_ref, wd_ref, o_ref, lt_scr, *, rp):
    i = pl.program_id(0)
    g = pl.program_id(1)
    tb = h_ref.shape[0]

    @pl.when((i == 0) & (g == 0))
    def _():
        r = lax.broadcasted_iota(jnp.int32, (tb, tb), 0)
        c = lax.broadcasted_iota(jnp.int32, (tb, tb), 1)
        lt_scr[...] = jnp.where(c < r, 1.0, 0.0).astype(BF16)

    @pl.when(g == 0)
    def _():
        o_ref[...] = x_ref[...]

    gf = g.astype(F32)
    gates = gate_ref[...]
    lane = lax.broadcasted_iota(jnp.int32, gates.shape, 1)
    gcol = jnp.sum(jnp.where(lane == GROUP_LANE, gates, 0.0), axis=1, keepdims=True)
    mcol = gcol == gf
    lt = lt_scr[...]
    ones_c = jnp.where(mcol, 1.0, 0.0).astype(BF16)
    rank_c = jnp.dot(lt, jnp.broadcast_to(ones_c, (tb, LANES)), preferred_element_type=F32)[:, 0:1]
    dcol = jnp.where(mcol, rank_c, -1.0)
    grow = grow_ref[0:1, :]
    mrow = grow == gf
    ones_r = jnp.broadcast_to(jnp.where(mrow, 1.0, 0.0).astype(BF16), (SUBLANES, tb))
    rank_r = lax.dot_general(ones_r, lt, (((1,), (1,)), ((), ())), preferred_element_type=F32)[0:1, :]
    drow = jnp.where(mrow, rank_r, -1.0)
    cnt = jnp.sum(jnp.where(mrow, 1.0, 0.0)).astype(jnp.int32)
    half = rp // 2
    rem = cnt % rp
    nbig = cnt // rp + (rem > half).astype(jnp.int32)

    g_hi = gates.astype(BF16)
    g_lo = (gates - g_hi.astype(F32)).astype(BF16)
    h = h_ref[...]

    def chunk(first_slot, rows):
        base = first_slot.astype(F32)
        lane_r = lax.broadcasted_iota(jnp.int32, (rows, LANES), 1)
        riota = lax.broadcasted_iota(jnp.int32, (rows, 1), 0).astype(F32)
        ciota = lax.broadcasted_iota(jnp.int32, (1, rows), 1).astype(F32)
        pk = jnp.where(drow == riota + base, 1.0, 0.0).astype(BF16)
        xs = jnp.dot(pk, h, preferred_element_type=F32).astype(BF16)
        gs = (jnp.dot(pk, g_hi, preferred_element_type=F32)
              + jnp.dot(pk, g_lo, preferred_element_type=F32))
        y = jnp.zeros((rows, o_ref.shape[1]), F32)
        for e in range(EXP_PER_GROUP):
            hg = jnp.dot(xs, wg_ref[e], preferred_element_type=F32)
            hu = jnp.dot(xs, wu_ref[e], preferred_element_type=F32)
            ge = jnp.sum(jnp.where(lane_r == g * EXP_PER_GROUP + e, gs, 0.0), axis=1, keepdims=True)
            act = (hg * jax.nn.sigmoid(hg)) * hu * ge
            y = y + jnp.dot(act.astype(BF16), wd_ref[e], preferred_element_type=F32)
        ptk = jnp.where(dcol == ciota + base, 1.0, 0.0).astype(BF16)
        o_ref[...] += jnp.dot(ptk, y.astype(BF16), preferred_element_type=F32)

    def big(k, carry):
        chunk(k * rp, rp)
        return carry

    lax.fori_loop(0, nbig, big, 0)

    @pl.when((rem > 0) & (rem <= half))
    def _():
        chunk(nbig * rp, half)


def _moe_sparse(x, h, gates, grow, wg_b, wu_b, wd_b, tb, rp):
    n, d = x.shape
    de = wg_b.shape[2]
    row = lambda i, g: (i, 0)
    kern = functools.partial(_moe_sparse_kernel, rp=rp)
    return pl.pallas_call(
        kern,
        grid=(n // tb, N_GROUPS),
        in_specs=[
            pl.BlockSpec((tb, d), row),
            pl.BlockSpec((tb, d), row),
            pl.BlockSpec((tb, LANES), row),
            pl.BlockSpec((SUBLANES, tb), lambda i, g: (0, i)),
            pl.BlockSpec((EXP_PER_GROUP, d, de), lambda i, g: (g, 0, 0)),
            pl.BlockSpec((EXP_PER_GROUP, d, de), lambda i, g: (g, 0, 0)),
            pl.BlockSpec((EXP_PER_GROUP, de, d), lambda i, g: (g, 0, 0)),
        ],
        out_specs=pl.BlockSpec((tb, d), row),
        out_shape=jax.ShapeDtypeStruct((n, d), F32),
        scratch_shapes=[pltpu.VMEM((tb, tb), BF16)],
        compiler_params=_cparams(2),
        name="moe_sparse",
    )(x, h, gates, grow, wg_b, wu_b, wd_b)


def _rope_tables(pos):
    half = ROT_DIM // 2
    inv = jnp.power(ROPE_THETA, -jnp.arange(0, ROT_DIM, 2, dtype=F32) / ROT_DIM)
    ang = pos.astype(F32)[:, None] * inv[None, :]
    cos, sin = jnp.cos(ang), jnp.sin(ang)
    n = pos.shape[0]
    pad = jnp.zeros((n, HEAD_DIM - ROT_DIM), F32)
    zer = jnp.zeros((n, half), F32)
    c = jnp.concatenate([cos, cos, pad + 1.0], axis=1)
    s1 = jnp.concatenate([-sin, zer, pad], axis=1)
    s2 = jnp.concatenate([zer, sin, pad], axis=1)
    rep = LANES // HEAD_DIM
    return jnp.tile(c, (1, rep)), jnp.tile(s1, (1, rep)), jnp.tile(s2, (1, rep))


def kernel(x_prompt, x_sample, cache_k, cache_v, state_conv, page_table, norm1_g, w_in, q_norm_g, k_norm_g,
           lam_q1, lam_k1, lam_q2, lam_k2, subln_g, conv_w, conv_b, conv_ln_g, conv_ln_b, w_out, norm2_g,
           w_router_group, w_router_expert, w_gate, w_up, w_down):
    batch, seq, d = x_prompt.shape
    db = x_sample.shape[0]
    depth = w_in.shape[0]
    n_pages, page = page_table.shape[1], cache_k.shape[2]
    past = n_pages * page
    n = batch * seq

    xp = x_prompt.reshape(n, d)
    xs = x_sample.reshape(db, d)
    tabs_p = _rope_tables(jnp.arange(seq, dtype=jnp.int32))
    tabs_s = _rope_tables(jnp.full((db,), past, dtype=jnp.int32))
    gi = jnp.arange(ATTN_WIDTH, dtype=jnp.int32) // HEAD_DIM
    bd = (gi[:, None] == gi[None, :]).astype(BF16)
    ckt = jnp.transpose(cache_k, (0, 1, 3, 4, 2))
    cvf = cache_v.reshape(depth, cache_v.shape[1], page * N_HEADS, V_DIM)
    state_t = jnp.transpose(state_conv, (0, 2, 1, 3))

    kp_l, vp_l, cp_l, ks_l, vs_l, cs_l = [], [], [], [], [], []
    for l in range(depth):
        lam_init = 0.8 - 0.6 * math.exp(-0.3 * l)
        g1 = norm1_g[l][None, :]
        w_in_b = w_in[l].astype(BF16)
        qg = jnp.tile(q_norm_g[l], N_SUB)[None, :]
        kg = jnp.tile(k_norm_g[l], N_SUB)[None, :]
        lam_p = jnp.stack([lam_q1[l], lam_k1[l], lam_q2[l], lam_k2[l]])
        sg = subln_g[l][None, :]
        cwp = jnp.pad(conv_w[l], ((0, HALO - CONV_W), (0, 0)))
        cb, lg, lb = conv_b[l][None, :], conv_ln_g[l][None, :], conv_ln_b[l][None, :]
        wo_b = w_out[l].astype(BF16)
        g2 = norm2_g[l][None, :]
        wr = jnp.concatenate([w_router_group[l], w_router_expert[l]], axis=1)
        wr_hi = wr.astype(BF16)
        wr_lo = (wr - wr_hi.astype(F32)).astype(BF16)
        nr = wr.shape[1]
        w1 = jnp.zeros((d, LANES), BF16).at[:, :nr].set(wr_hi).at[:, LANES // 2:LANES // 2 + nr].set(wr_lo)
        w2 = jnp.zeros((d, LANES), BF16).at[:, :nr].set(wr_hi)
        wg_b, wu_b, wd_b = w_gate[l].astype(BF16), w_up[l].astype(BF16), w_down[l].astype(BF16)

        q, k, v, u, kb = _inproj(xp, g1, w_in_b, bd, qg, kg, *tabs_p, tm=512)
        att = _prompt_attention(q, kb, v, lam_p, sg.reshape(V_DIM, 1), batch, seq, 512, lam_init)
        xp, h2, gates, grow = _mix_prompt(xp, att, u, cwp, cb, lg, lb, wo_b, g2, w1, w2, seq, 256)
        xp = _moe_sparse(xp, h2, gates, grow, wg_b, wu_b, wd_b, 1024, 256)
        kp_l.append(k.reshape(batch, seq, N_SUB, HEAD_DIM))
        vp_l.append(v.reshape(batch, seq, N_HEADS, V_DIM))
        cp_l.append(u.reshape(batch, seq, CONV_CH)[:, seq - CONV_BUF:])

        q, k, v, u, _ = _inproj(xs, g1, w_in_b, bd, qg, kg, *tabs_s, tm=db)
        q8 = q.astype(F32).reshape(db, N_SUB, HEAD_DIM)
        vn8 = jnp.repeat(v.reshape(db, N_HEADS, V_DIM), 2, axis=1)
        att = _decode_attention(page_table, q8, jnp.transpose(q8, (0, 2, 1)), k.reshape(db, N_SUB, HEAD_DIM),
                                vn8, lam_p, sg, ckt, cvf, l, lam_init)
        att = att.reshape(db, ATTN_WIDTH).astype(BF16)
        xs, h2, gates, ns = _mix_decode(xs, att, u, state_t, l, cwp, cb, lg, lb, wo_b, g2, w1, w2, 64)
        xs = _moe(xs, h2, gates, wg_b, wu_b, wd_b, db)
        ks_l.append(k.reshape(db, 1, N_SUB, HEAD_DIM))
        vs_l.append(v.reshape(db, 1, N_HEADS, V_DIM))
        cs_l.append(jnp.transpose(ns, (1, 0, 2)))

    return (xp.reshape(batch, seq, d), xs.reshape(db, 1, d), jnp.stack(kp_l), jnp.stack(vp_l), jnp.stack(cp_l),
            jnp.stack(ks_l), jnp.stack(vs_l), jnp.stack(cs_l))
```

```python
import functools
import math

import jax
import jax.numpy as jnp
from jax import lax
from jax.experimental import pallas as pl
from jax.experimental.pallas import tpu as pltpu

F32 = jnp.float32
BF16 = jnp.bfloat16

D_MODEL = 1024
HEAD_DIM = 64
N_SUB = 8
N_HEADS = 4
V_DIM = 128
ATTN_WIDTH = 512
CONV_CH = 512
ROT_DIM = 16
ROPE_THETA = 500000.0
CONV_W = 31
CONV_BUF = CONV_W - 1
N_GROUPS = 4
EXP_PER_GROUP = 4
N_EXPERTS = 16
D_EXPERT = 512
EPS = 1e-6
D_IN = 3 * ATTN_WIDTH + 2 * CONV_CH

LANES = 128
SUBLANES = 8
HALO = 32
EXT_TAIL = 16
Q_SCALE = HEAD_DIM ** -0.5 * math.log2(math.e)
QK_AHEAD = 2
GROUP_LANE = N_EXPERTS
VMEM_LIMIT = 56 * 1024 * 1024


def _cparams(n_axes):
    return pltpu.CompilerParams(dimension_semantics=("arbitrary",) * n_axes,
                                vmem_limit_bytes=VMEM_LIMIT)


def _inproj_kernel(x_ref, g1_ref, w_ref, bd_ref, qg_ref, kg_ref, c_ref, s1_ref, s2_ref, *rest):
    q_ref, k_ref, v_ref, u_ref, kb_ref = rest[-5:]
    x = x_ref[...]
    ms = jnp.mean(x * x, axis=-1, keepdims=True)
    h = (x * lax.rsqrt(ms + EPS) * g1_ref[...]).astype(BF16)
    z = jnp.dot(h, w_ref[...], preferred_element_type=F32)
    c, s1, s2 = c_ref[...], s1_ref[...], s2_ref[...]
    bd = bd_ref[...]

    def head_norm_rope(t, g):
        hms = jnp.dot((t * t).astype(BF16), bd, preferred_element_type=F32) * (1.0 / HEAD_DIM)
        tn = t * lax.rsqrt(hms + EPS) * g
        outs = []
        for j in range(ATTN_WIDTH // LANES):
            blk = tn[:, j * LANES:(j + 1) * LANES]
            up = pltpu.roll(blk, LANES - ROT_DIM // 2, 1)
            dn = pltpu.roll(blk, ROT_DIM // 2, 1)
            outs.append(blk * c + up * s1 + dn * s2)
        return outs

    qs = head_norm_rope(z[:, 0:ATTN_WIDTH], qg_ref[...])
    ks = head_norm_rope(z[:, ATTN_WIDTH:2 * ATTN_WIDTH], kg_ref[...])
    for j in range(ATTN_WIDTH // LANES):
        sl = slice(j * LANES, (j + 1) * LANES)
        q_ref[:, sl] = (qs[j] * Q_SCALE).astype(BF16)
        k_ref[sl, :] = ks[j].T
        kb_ref[:, sl] = ks[j].astype(BF16)
    v_ref[...] = z[:, 2 * ATTN_WIDTH:3 * ATTN_WIDTH]
    a = z[:, 3 * ATTN_WIDTH:3 * ATTN_WIDTH + CONV_CH]
    gt = z[:, 3 * ATTN_WIDTH + CONV_CH:]
    u_ref[...] = a * jax.nn.sigmoid(gt)


def _inproj(x, layer, depth, kv_prev, g1, w_in_b, bd, qg, kg, ct, s1t, s2t, tm):
    n = x.shape[0]
    npos = ct.shape[0] // tm
    row = lambda i: (i, 0)
    full = lambda i: (0, 0)
    pos = lambda i: (i % npos, 0)
    lrow = lambda i: (layer, i, 0)
    in_specs = [
        pl.BlockSpec((tm, D_MODEL), row),
        pl.BlockSpec((1, D_MODEL), full),
        pl.BlockSpec((None, D_MODEL, D_IN), lambda i: (layer, 0, 0)),
        pl.BlockSpec((ATTN_WIDTH, ATTN_WIDTH), full),
        pl.BlockSpec((1, ATTN_WIDTH), full),
        pl.BlockSpec((1, ATTN_WIDTH), full),
        pl.BlockSpec((tm, LANES), pos),
        pl.BlockSpec((tm, LANES), pos),
        pl.BlockSpec((tm, LANES), pos),
    ]
    args = [x, g1, w_in_b, bd, qg, kg, ct, s1t, s2t]
    aliases = {}
    if kv_prev is not None:
        aliases = {len(args): 1, len(args) + 1: 2}
        in_specs += [pl.BlockSpec(memory_space=pl.ANY)] * 2
        args += list(kv_prev)
    return pl.pallas_call(
        _inproj_kernel,
        grid=(n // tm,),
        in_specs=in_specs,
        out_specs=[pl.BlockSpec((tm, ATTN_WIDTH), row),
                   pl.BlockSpec((None, None, ATTN_WIDTH, tm), lambda i: (layer, i // npos, 0, i % npos)),
                   pl.BlockSpec((None, tm, ATTN_WIDTH), lrow),
                   pl.BlockSpec((tm, CONV_CH), row),
                   pl.BlockSpec((tm, ATTN_WIDTH), row)],
        out_shape=[
            jax.ShapeDtypeStruct((n, ATTN_WIDTH), BF16),
            jax.ShapeDtypeStruct((depth, n // (npos * tm), ATTN_WIDTH, npos * tm), F32),
            jax.ShapeDtypeStruct((depth, n, ATTN_WIDTH), F32),
            jax.ShapeDtypeStruct((n, CONV_CH), F32),
            jax.ShapeDtypeStruct((n, ATTN_WIDTH), BF16),
        ],
        input_output_aliases=aliases,
        compiler_params=_cparams(1),
        name="inproj",
    )(*args)


def _lambda_value(lam_ref, lam_init):
    lp = lam_ref[...]
    t1 = jnp.sum(lp[0:1] * lp[1:2], axis=1, keepdims=True)
    t2 = jnp.sum(lp[2:3] * lp[3:4], axis=1, keepdims=True)
    return jnp.exp(t1) - jnp.exp(t2) + lam_init


def _subln(o, g, lam_init):
    ms = jnp.mean(o * o, axis=-1, keepdims=True)
    return o * lax.rsqrt(ms + EPS) * g * (1.0 - lam_init)


def _pattn_kernel(q_ref, k_ref, v_ref, lam_ref, sgc_ref, o_ref, vt_scr, qs_scr, m_scr, l_scr, acc_scr,
                  *, tq, lam_init):
    i = pl.program_id(1)
    nkv = k_ref.shape[0] // tq

    @pl.when(i == 0)
    def _():
        for c in range(nkv):
            vt_scr[c] = v_ref[c * tq:(c + 1) * tq, :].T.astype(BF16)

    lam = _lambda_value(lam_ref, lam_init)
    lane = lax.broadcasted_iota(jnp.int32, (tq, LANES), 1)
    causal = (lax.broadcasted_iota(jnp.int32, (tq, tq), 0) <= lax.broadcasted_iota(jnp.int32, (tq, tq), 1))

    for sh in range(N_SUB):
        qp = q_ref[:, (sh // 2) * LANES:(sh // 2 + 1) * LANES]
        keep = (lane < HEAD_DIM) if sh % 2 == 0 else (lane >= HEAD_DIM)
        qs_scr[sh] = jnp.where(keep, qp, jnp.zeros_like(qp))
        m_scr[sh] = jnp.full((1, tq), -jnp.inf, F32)
        l_scr[sh] = jnp.zeros((1, tq), F32)
        acc_scr[sh] = jnp.zeros((V_DIM, tq), F32)

    def step(j, masked):
        off = pl.multiple_of(j * tq, tq)

        def scores(sh):
            kj = k_ref[pl.ds(off, tq), (sh // 2) * LANES:(sh // 2 + 1) * LANES]
            return lax.dot_general(kj, qs_scr[sh], (((1,), (1,)), ((), ())), preferred_element_type=F32)

        pending = [scores(sh) for sh in range(QK_AHEAD)]
        for sh in range(N_SUB):
            sl = slice((sh // 2) * LANES, (sh // 2 + 1) * LANES)
            st = pending.pop(0)
            if sh + QK_AHEAD < N_SUB:
                pending.append(scores(sh + QK_AHEAD))
            if masked:
                st = jnp.where(causal, st, -jnp.inf)
            m_prev = m_scr[sh]
            m_new = jnp.maximum(m_prev, jnp.max(st, axis=0, keepdims=True))
            alpha = jnp.exp2(m_prev - m_new)
            p = jnp.exp2(st - m_new)
            l_scr[sh] = alpha * l_scr[sh] + jnp.sum(p, axis=0, keepdims=True)
            acc_scr[sh] = alpha * acc_scr[sh] + jnp.dot(vt_scr[j, sl, :], p.astype(BF16),
                                                        preferred_element_type=F32)
            m_scr[sh] = m_new

    def body(j, carry):
        step(j, False)
        return carry

    lax.fori_loop(0, i, body, 0)
    step(i, True)

    sgc = sgc_ref[...]
    for hd in range(N_HEADS):
        od = acc_scr[2 * hd] / l_scr[2 * hd] - lam * (acc_scr[2 * hd + 1] / l_scr[2 * hd + 1])
        ms = jnp.mean(od * od, axis=0, keepdims=True)
        on = od * lax.rsqrt(ms + EPS) * sgc * (1.0 - lam_init)
        o_ref[:, hd * LANES:(hd + 1) * LANES] = on.T.astype(BF16)


def _prompt_attention(q, kb, v_all, layer, lam_p, sgc, batch, seq, tq, lam_init):
    nq = seq // tq
    n = batch * seq
    kern = functools.partial(_pattn_kernel, tq=tq, lam_init=lam_init)
    return pl.pallas_call(
        kern,
        grid=(batch, nq),
        in_specs=[
            pl.BlockSpec((tq, ATTN_WIDTH), lambda b, i: (b * nq + i, 0)),
            pl.BlockSpec((seq, ATTN_WIDTH), lambda b, i: (b, 0)),
            pl.BlockSpec((None, seq, ATTN_WIDTH), lambda b, i: (layer, b, 0)),
            pl.BlockSpec((4, HEAD_DIM), lambda b, i: (0, 0)),
            pl.BlockSpec((V_DIM, 1), lambda b, i: (0, 0)),
        ],
        out_specs=pl.BlockSpec((tq, ATTN_WIDTH), lambda b, i: (b * nq + i, 0)),
        out_shape=jax.ShapeDtypeStruct((n, ATTN_WIDTH), BF16),
        scratch_shapes=[
            pltpu.VMEM((nq, ATTN_WIDTH, tq), BF16),
            pltpu.VMEM((N_SUB, tq, LANES), BF16),
            pltpu.VMEM((N_SUB, 1, tq), F32),
            pltpu.VMEM((N_SUB, 1, tq), F32),
            pltpu.VMEM((N_SUB, V_DIM, tq), F32),
        ],
        compiler_params=_cparams(2),
        name="prompt_attn",
    )(q, kb, v_all, lam_p, sgc)


def _dattn_kernel(pt_ref, q_ref, qt_ref, kn_ref, vn_ref, lam_ref, sg_ref, *rest, n_pages, page, lam_init):
    k_refs = rest[:n_pages]
    v_refs = rest[n_pages:2 * n_pages]
    o_ref = rest[2 * n_pages]
    del pt_ref
    lam = _lambda_value(lam_ref, lam_init)
    qt = qt_ref[...]
    qcols = [jnp.broadcast_to(qt[:, h:h + 1], (HEAD_DIM, page)) for h in range(N_SUB)]
    s = jnp.concatenate(
        [jnp.concatenate([jnp.sum(k_refs[p][h] * qcols[h], axis=0, keepdims=True) for h in range(N_SUB)], axis=0)
         for p in range(n_pages)], axis=1)
    s_self = jnp.sum(q_ref[...] * kn_ref[...], axis=1, keepdims=True)
    m = jnp.maximum(jnp.max(s, axis=1, keepdims=True), s_self)
    pexp = jnp.exp2(s - m)
    pself = jnp.exp2(s_self - m)
    inv = 1.0 / (jnp.sum(pexp, axis=1, keepdims=True) + pself)
    sub = lax.broadcasted_iota(jnp.int32, (N_SUB, 1), 0)
    coef = jnp.where((sub & 1) == 0, 1.0, -lam) * inv
    a8 = pexp * coef
    a8 = (a8 + pltpu.roll(a8, N_SUB - 1, 0)).astype(BF16)
    as8 = jnp.broadcast_to(pself * coef, (N_SUB, V_DIM))
    as8 = as8 + pltpu.roll(as8, N_SUB - 1, 0)
    o_self = as8 * vn_ref[...]
    rows = []
    for hd in range(N_HEADS):
        acc = jnp.zeros((N_SUB, V_DIM), F32)
        for p in range(n_pages):
            v_hd = v_refs[p][pl.ds(hd, page, stride=N_HEADS), :].astype(BF16)
            acc = acc + jnp.dot(a8[:, p * page:(p + 1) * page], v_hd, preferred_element_type=F32)
        rows.append(acc[2 * hd:2 * hd + 1] + o_self[2 * hd:2 * hd + 1])
    o = jnp.concatenate(rows, axis=0)
    o_ref[...] = _subln(o, sg_ref[...], lam_init)


def _decode_attention(page_table, q8, qt, kn8, vn8, lam_p, sg, ckt, cvf, layer, lam_init):
    db, n_pages = page_table.shape
    page = ckt.shape[-1]
    kern = functools.partial(_dattn_kernel, n_pages=n_pages, page=page, lam_init=lam_init)
    const2 = lambda b, pt: (0, 0)

    def kmap(p):
        return lambda b, pt: (layer, pt[b, p], 0, 0, 0)

    def vmap_(p):
        return lambda b, pt: (layer, pt[b, p], 0, 0)

    in_specs = [
        pl.BlockSpec((None, N_SUB, HEAD_DIM), lambda b, pt: (b, 0, 0)),
        pl.BlockSpec((None, HEAD_DIM, N_SUB), lambda b, pt: (b, 0, 0)),
        pl.BlockSpec((None, N_SUB, HEAD_DIM), lambda b, pt: (b, 0, 0)),
        pl.BlockSpec((None, N_SUB, V_DIM), lambda b, pt: (b, 0, 0)),
        pl.BlockSpec((4, HEAD_DIM), const2),
        pl.BlockSpec((1, V_DIM), const2),
    ]
    in_specs += [pl.BlockSpec((None, None, N_SUB, HEAD_DIM, page), kmap(p)) for p in range(n_pages)]
    in_specs += [pl.BlockSpec((None, None, page * N_HEADS, V_DIM), vmap_(p)) for p in range(n_pages)]
    grid_spec = pltpu.PrefetchScalarGridSpec(
        num_scalar_prefetch=1,
        grid=(db,),
        in_specs=in_specs,
        out_specs=pl.BlockSpec((None, N_HEADS, V_DIM), lambda b, pt: (b, 0, 0)),
    )
    return pl.pallas_call(
        kern,
        grid_spec=grid_spec,
        out_shape=jax.ShapeDtypeStruct((db, N_HEADS, V_DIM), F32),
        compiler_params=_cparams(1),
        name="decode_attn",
    )(page_table, q8, qt, kn8, vn8, lam_p, sg, *([ckt] * n_pages), *([cvf] * n_pages))


def _conv_post(y, lg_ref, lb_ref):
    mu = jnp.mean(y, axis=-1, keepdims=True)
    yc = y - mu
    var = jnp.mean(yc * yc, axis=-1, keepdims=True)
    yn = yc * lax.rsqrt(var + EPS) * lg_ref[...] + lb_ref[...]
    return yn * jax.nn.sigmoid(yn)


def _mix_tail(x, att, cnv, wo_ref, g2_ref, w1_ref, w2_ref, xo_ref, h_ref, gate_ref, grow_ref=None):
    y = (x + jnp.dot(att, wo_ref[0:ATTN_WIDTH, :], preferred_element_type=F32)
         + jnp.dot(cnv.astype(BF16), wo_ref[ATTN_WIDTH:, :], preferred_element_type=F32))
    xo_ref[...] = y
    ms = jnp.mean(y * y, axis=-1, keepdims=True)
    h = y * lax.rsqrt(ms + EPS) * g2_ref[...]
    h_hi = h.astype(BF16)
    h_lo = (h - h_hi.astype(F32)).astype(BF16)
    h_ref[...] = h_hi
    r = jnp.dot(h_hi, w1_ref[...], preferred_element_type=F32)
    lg = r + pltpu.roll(r, LANES // 2, 1) + jnp.dot(h_lo, w2_ref[...], preferred_element_type=F32)
    tm = lg.shape[0]
    lane = lax.broadcasted_iota(jnp.int32, (tm, LANES), 1)
    lanef = lane.astype(F32)
    big = float(LANES)
    neg = -jnp.inf
    gmask = lane < N_GROUPS
    gl = jnp.where(gmask, lg, neg)
    gmax = jnp.max(gl, axis=1, keepdims=True)
    gsum = jnp.sum(jnp.where(gmask, jnp.exp(gl - gmax), 0.0), axis=1, keepdims=True)
    g_w = 1.0 / gsum
    g_idx = jnp.min(jnp.where(gl == gmax, lanef, big), axis=1, keepdims=True)
    lo = N_GROUPS + EXP_PER_GROUP * g_idx
    emask = (lanef >= lo) & (lanef < lo + EXP_PER_GROUP)
    ev = jnp.where(emask, lg, neg)
    v1 = jnp.max(ev, axis=1, keepdims=True)
    i1 = jnp.min(jnp.where(ev == v1, lanef, big), axis=1, keepdims=True)
    ev2 = jnp.where(lanef == i1, neg, ev)
    v2 = jnp.max(ev2, axis=1, keepdims=True)
    i2 = jnp.min(jnp.where(ev2 == v2, lanef, big), axis=1, keepdims=True)
    e2 = jnp.exp(v2 - v1)
    den = 1.0 / (1.0 + e2)
    w1 = den * g_w
    w2 = e2 * den * g_w
    gates = jnp.where(lanef == i1, w1, jnp.where(lanef == i2, w2, 0.0))
    gates = pltpu.roll(gates, LANES - N_GROUPS, 1)
    gate_ref[...] = jnp.where(lane == GROUP_LANE, g_idx, gates)
    if grow_ref is not None:
        grow_ref[...] = jnp.transpose(jnp.broadcast_to(g_idx, (tm, LANES)))[0:SUBLANES, :]


def _mix_prompt_kernel(x_ref, att_ref, u_ref, uh_ref, cw_ref, cb_ref, lg_ref, lb_ref, wo_ref, g2_ref,
                       w1_ref, w2_ref, xo_ref, h_ref, gate_ref, grow_ref, ext_scr, *, tm, tiles_per_seq):
    i = pl.program_id(0)
    first = (i % tiles_per_seq) == 0
    ext_scr[0:HALO, :] = jnp.where(first, 0.0, uh_ref[...])
    ext_scr[HALO:HALO + tm, :] = u_ref[...]
    ext_scr[HALO + tm:, :] = jnp.zeros((EXT_TAIL, CONV_CH), F32)
    lead = HALO - CONV_BUF
    acc = jnp.zeros((tm, CONV_CH), F32) + cb_ref[...]
    for r in range(SUBLANES):
        z = None
        for a in range((CONV_W + lead + SUBLANES - 1) // SUBLANES):
            j = SUBLANES * a + r - lead
            if 0 <= j < CONV_W:
                term = cw_ref[j:j + 1, :] * ext_scr[SUBLANES * a:SUBLANES * a + tm + SUBLANES, :]
                z = term if z is None else z + term
        acc = acc + z[r:r + tm]
    cnv = _conv_post(acc, lg_ref, lb_ref)
    _mix_tail(x_ref[...], att_ref[...], cnv, wo_ref, g2_ref, w1_ref, w2_ref, xo_ref, h_ref, gate_ref, grow_ref)


def _mix_decode_kernel(x_ref, att_ref, u_ref, st_ref, cw_ref, cb_ref, lg_ref, lb_ref, wo_ref, g2_ref,
                       w1_ref, w2_ref, xo_ref, h_ref, gate_ref, ns_ref):
    u = u_ref[...]
    acc = cb_ref[...] + cw_ref[CONV_BUF:CONV_BUF + 1, :] * u
    for j in range(CONV_BUF):
        acc = acc + cw_ref[j:j + 1, :] * st_ref[j]
    for j in range(CONV_BUF - 1):
        ns_ref[j] = st_ref[j + 1]
    ns_ref[CONV_BUF - 1] = u
    cnv = _conv_post(acc, lg_ref, lb_ref)
    _mix_tail(x_ref[...], att_ref[...], cnv, wo_ref, g2_ref, w1_ref, w2_ref, xo_ref, h_ref, gate_ref)


def _mix_common_specs(tm, layer):
    row = lambda i: (i, 0)
    full = lambda i: (0, 0)
    tail_in = [
        pl.BlockSpec((HALO, CONV_CH), full),
        pl.BlockSpec((1, CONV_CH), full),
        pl.BlockSpec((1, CONV_CH), full),
        pl.BlockSpec((1, CONV_CH), full),
        pl.BlockSpec((None, D_MODEL, D_MODEL), lambda i: (layer, 0, 0)),
        pl.BlockSpec((1, D_MODEL), full),
        pl.BlockSpec((D_MODEL, LANES), full),
        pl.BlockSpec((D_MODEL, LANES), full),
    ]
    out_specs = [pl.BlockSpec((tm, D_MODEL), row), pl.BlockSpec((tm, D_MODEL), row),
                 pl.BlockSpec((tm, LANES), row)]
    return row, tail_in, out_specs


def _mix_out_shape(n):
    return [jax.ShapeDtypeStruct((n, D_MODEL), F32), jax.ShapeDtypeStruct((n, D_MODEL), BF16),
            jax.ShapeDtypeStruct((n, LANES), F32)]


def _mix_prompt(x, att, u, layer, cwp, cb, lg, lb, wo_b, g2, w1, w2, seq, tm):
    n = x.shape[0]
    row, tail_in, out_specs = _mix_common_specs(tm, layer)
    halo_map = lambda i: (jnp.maximum(i * (tm // HALO) - 1, 0), 0)
    kern = functools.partial(_mix_prompt_kernel, tm=tm, tiles_per_seq=seq // tm)
    return pl.pallas_call(
        kern,
        grid=(n // tm,),
        in_specs=[pl.BlockSpec((tm, D_MODEL), row), pl.BlockSpec((tm, ATTN_WIDTH), row),
                  pl.BlockSpec((tm, CONV_CH), row), pl.BlockSpec((HALO, CONV_CH), halo_map)] + tail_in,
        out_specs=out_specs + [pl.BlockSpec((SUBLANES, tm), lambda i: (0, i))],
        out_shape=_mix_out_shape(n) + [jax.ShapeDtypeStruct((SUBLANES, n), F32)],
        scratch_shapes=[pltpu.VMEM((tm + HALO + EXT_TAIL, CONV_CH), F32)],
        compiler_params=_cparams(1),
        name="mix_prompt",
    )(x, att, u, u, cwp, cb, lg, lb, wo_b, g2, w1, w2)


def _mix_decode(x, att, u, state_t, layer, cwp, cb, lg, lb, wo_b, g2, w1, w2, tm):
    n = x.shape[0]
    row, tail_in, out_specs = _mix_common_specs(tm, layer)
    return pl.pallas_call(
        _mix_decode_kernel,
        grid=(n // tm,),
        in_specs=[pl.BlockSpec((tm, D_MODEL), row), pl.BlockSpec((tm, ATTN_WIDTH), row),
                  pl.BlockSpec((tm, CONV_CH), row),
                  pl.BlockSpec((None, CONV_BUF, tm, CONV_CH), lambda i: (layer, 0, i, 0))] + tail_in,
        out_specs=out_specs + [pl.BlockSpec((CONV_BUF, tm, CONV_CH), lambda i: (0, i, 0))],
        out_shape=_mix_out_shape(n) + [jax.ShapeDtypeStruct((CONV_BUF, n, CONV_CH), F32)],
        compiler_params=_cparams(1),
        name="mix_decode",
    )(x, att, u, state_t, cwp, cb, lg, lb, wo_b, g2, w1, w2)


def _moe_kernel(x_ref, h_ref, gate_ref, wg_ref, wu_ref, wd_ref, o_ref):
    e = pl.program_id(1)

    @pl.when(e == 0)
    def _():
        o_ref[...] = x_ref[...]

    h = h_ref[...]
    hg = jnp.dot(h, wg_ref[...], preferred_element_type=F32)
    hu = jnp.dot(h, wu_ref[...], preferred_element_type=F32)
    gates = gate_ref[...]
    lane = lax.broadcasted_iota(jnp.int32, gates.shape, 1)
    ge = jnp.sum(jnp.where(lane == e, gates, 0.0), axis=1, keepdims=True)
    act = (hg * jax.nn.sigmoid(hg)) * hu * ge
    o_ref[...] += jnp.dot(act.astype(BF16), wd_ref[...], preferred_element_type=F32)


def _moe(x, h, gates, layer, wg_b, wu_b, wd_b, tm):
    n = x.shape[0]
    row = lambda i, e: (i, 0)
    return pl.pallas_call(
        _moe_kernel,
        grid=(n // tm, N_EXPERTS),
        in_specs=[
            pl.BlockSpec((tm, D_MODEL), row),
            pl.BlockSpec((tm, D_MODEL), row),
            pl.BlockSpec((tm, LANES), row),
            pl.BlockSpec((None, None, D_MODEL, D_EXPERT), lambda i, e: (layer, e, 0, 0)),
            pl.BlockSpec((None, None, D_MODEL, D_EXPERT), lambda i, e: (layer, e, 0, 0)),
            pl.BlockSpec((None, None, D_EXPERT, D_MODEL), lambda i, e: (layer, e, 0, 0)),
        ],
        out_specs=pl.BlockSpec((tm, D_MODEL), row),
        out_shape=jax.ShapeDtypeStruct((n, D_MODEL), F32),
        compiler_params=_cparams(2),
        name="moe",
    )(x, h, gates, wg_b, wu_b, wd_b)


def _moe_sparse_kernel(x_ref, h_ref, gate_ref, grow_ref, wg_ref, wu_ref, wd_ref, o_ref, lt_scr, *, rp):
    i = pl.program_id(0)
    g = pl.program_id(1)
    tb = h_ref.shape[0]

    @pl.when((i == 0) & (g == 0))
    def _():
        r = lax.broadcasted_iota(jnp.int32, (tb, tb), 0)
        c = lax.broadcasted_iota(jnp.int32, (tb, tb), 1)
        lt_scr[...] = jnp.where(c < r, 1.0, 0.0).astype(BF16)

    @pl.when(g == 0)
    def _():
        o_ref[...] = x_ref[...]

    gf = g.astype(F32)
    gates = gate_ref[...]
    lane = lax.broadcasted_iota(jnp.int32, gates.shape, 1)
    gcol = jnp.sum(jnp.where(lane == GROUP_LANE, gates, 0.0), axis=1, keepdims=True)
    mcol = gcol == gf
    lt = lt_scr[...]
    ones_c = jnp.where(mcol, 1.0, 0.0).astype(BF16)
    rank_c = jnp.dot(lt, jnp.broadcast_to(ones_c, (tb, LANES)), preferred_element_type=F32)[:, 0:1]
    dcol = jnp.where(mcol, rank_c, -1.0)
    grow = grow_ref[0:1, :]
    mrow = grow == gf
    ones_r = jnp.broadcast_to(jnp.where(mrow, 1.0, 0.0).astype(BF16), (SUBLANES, tb))
    rank_r = lax.dot_general(ones_r, lt, (((1,), (1,)), ((), ())), preferred_element_type=F32)[0:1, :]
    drow = jnp.where(mrow, rank_r, -1.0)
    cnt = jnp.sum(jnp.where(mrow, 1.0, 0.0)).astype(jnp.int32)
    half = rp // 2
    rem = cnt % rp
    nbig = cnt // rp + (rem > half).astype(jnp.int32)

    g_hi = gates.astype(BF16)
    g_lo = (gates - g_hi.astype(F32)).astype(BF16)
    h = h_ref[...]

    def chunk(first_slot, rows):
        base = first_slot.astype(F32)
        lane_r = lax.broadcasted_iota(jnp.int32, (rows, LANES), 1)
        riota = lax.broadcasted_iota(jnp.int32, (rows, 1), 0).astype(F32)
        ciota = lax.broadcasted_iota(jnp.int32, (1, rows), 1).astype(F32)
        pk = jnp.where(drow == riota + base, 1.0, 0.0).astype(BF16)
        xs = jnp.dot(pk, h, preferred_element_type=F32).astype(BF16)
        gs = (jnp.dot(pk, g_hi, preferred_element_type=F32)
              + jnp.dot(pk, g_lo, preferred_element_type=F32))
        y = jnp.zeros((rows, o_ref.shape[1]), F32)
        for e in range(EXP_PER_GROUP):
            hg = jnp.dot(xs, wg_ref[e], preferred_element_type=F32)
            hu = jnp.dot(xs, wu_ref[e], preferred_element_type=F32)
            ge = jnp.sum(jnp.where(lane_r == g * EXP_PER_GROUP + e, gs, 0.0), axis=1, keepdims=True)
            act = (hg * jax.nn.sigmoid(hg)) * hu * ge
            y = y + jnp.dot(act.astype(BF16), wd_ref[e], preferred_element_type=F32)
        ptk = jnp.where(dcol == ciota + base, 1.0, 0.0).astype(BF16)
        o_ref[...] += jnp.dot(ptk, y.astype(BF16), preferred_element_type=F32)

    def big(k, carry):
        chunk(k * rp, rp)
        return carry

    lax.fori_loop(0, nbig, big, 0)

    @pl.when((rem > 0) & (rem <= half))
    def _():
        chunk(nbig * rp, half)


def _moe_sparse(x, h, gates, grow, layer, wg_b, wu_b, wd_b, tb, rp):
    n, d = x.shape
    de = wg_b.shape[-1]
    row = lambda i, g: (i, 0)
    kern = functools.partial(_moe_sparse_kernel, rp=rp)
    return pl.pallas_call(
        kern,
        grid=(n // tb, N_GROUPS),
        in_specs=[
            pl.BlockSpec((tb, d), row),
            pl.BlockSpec((tb, d), row),
            pl.BlockSpec((tb, LANES), row),
            pl.BlockSpec((SUBLANES, tb), lambda i, g: (0, i)),
            pl.BlockSpec((None, EXP_PER_GROUP, d, de), lambda i, g: (layer, g, 0, 0)),
            pl.BlockSpec((None, EXP_PER_GROUP, d, de), lambda i, g: (layer, g, 0, 0)),
            pl.BlockSpec((None, EXP_PER_GROUP, de, d), lambda i, g: (layer, g, 0, 0)),
        ],
        out_specs=pl.BlockSpec((tb, d), row),
        out_shape=jax.ShapeDtypeStruct((n, d), F32),
        scratch_shapes=[pltpu.VMEM((tb, tb), BF16)],
        compiler_params=_cparams(2),
        name="moe_sparse",
    )(x, h, gates, grow, wg_b, wu_b, wd_b)


def _rope_tables(pos):
    half = ROT_DIM // 2
    inv = jnp.power(ROPE_THETA, -jnp.arange(0, ROT_DIM, 2, dtype=F32) / ROT_DIM)
    ang = pos.astype(F32)[:, None] * inv[None, :]
    cos, sin = jnp.cos(ang), jnp.sin(ang)
    n = pos.shape[0]
    pad = jnp.zeros((n, HEAD_DIM - ROT_DIM), F32)
    zer = jnp.zeros((n, half), F32)
    c = jnp.concatenate([cos, cos, pad + 1.0], axis=1)
    s1 = jnp.concatenate([-sin, zer, pad], axis=1)
    s2 = jnp.concatenate([zer, sin, pad], axis=1)
    rep = LANES // HEAD_DIM
    return jnp.tile(c, (1, rep)), jnp.tile(s1, (1, rep)), jnp.tile(s2, (1, rep))


def kernel(x_prompt, x_sample, cache_k, cache_v, state_conv, page_table, norm1_g, w_in, q_norm_g, k_norm_g,
           lam_q1, lam_k1, lam_q2, lam_k2, subln_g, conv_w, conv_b, conv_ln_g, conv_ln_b, w_out, norm2_g,
           w_router_group, w_router_expert, w_gate, w_up, w_down):
    batch, seq, d = x_prompt.shape
    db = x_sample.shape[0]
    depth = w_in.shape[0]
    n_pages, page = page_table.shape[1], cache_k.shape[2]
    past = n_pages * page
    n = batch * seq

    xp = x_prompt.reshape(n, d)
    xs = x_sample.reshape(db, d)
    tabs_p = _rope_tables(jnp.arange(seq, dtype=jnp.int32))
    tabs_s = _rope_tables(jnp.full((db,), past, dtype=jnp.int32))
    gi = jnp.arange(ATTN_WIDTH, dtype=jnp.int32) // HEAD_DIM
    bd = (gi[:, None] == gi[None, :]).astype(BF16)
    ckt = jnp.transpose(cache_k, (0, 1, 3, 4, 2))
    cvf = cache_v.reshape(depth, cache_v.shape[1], page * N_HEADS, V_DIM)
    state_t = jnp.transpose(state_conv, (0, 2, 1, 3))
    w_in_b, wo_b = w_in.astype(BF16), w_out.astype(BF16)
    wg_b, wu_b, wd_b = w_gate.astype(BF16), w_up.astype(BF16), w_down.astype(BF16)
    kvp = kvs = None

    cp_l, cs_l = [], []
    for l in range(depth):
        lam_init = 0.8 - 0.6 * math.exp(-0.3 * l)
        g1 = norm1_g[l][None, :]
        qg = jnp.tile(q_norm_g[l], N_SUB)[None, :]
        kg = jnp.tile(k_norm_g[l], N_SUB)[None, :]
        lam_p = jnp.stack([lam_q1[l], lam_k1[l], lam_q2[l], lam_k2[l]])
        sg = subln_g[l][None, :]
        cwp = jnp.pad(conv_w[l], ((0, HALO - CONV_W), (0, 0)))
        cb, lg, lb = conv_b[l][None, :], conv_ln_g[l][None, :], conv_ln_b[l][None, :]
        g2 = norm2_g[l][None, :]
        wr = jnp.concatenate([w_router_group[l], w_router_expert[l]], axis=1)
        wr_hi = wr.astype(BF16)
        wr_lo = (wr - wr_hi.astype(F32)).astype(BF16)
        nr = wr.shape[1]
        w1 = jnp.zeros((d, LANES), BF16).at[:, :nr].set(wr_hi).at[:, LANES // 2:LANES // 2 + nr].set(wr_lo)
        w2 = jnp.zeros((d, LANES), BF16).at[:, :nr].set(wr_hi)

        q, k_all, v_all, u, kb = _inproj(xp, l, depth, kvp, g1, w_in_b, bd, qg, kg, *tabs_p, tm=512)
        kvp = (k_all, v_all)
        att = _prompt_attention(q, kb, v_all, l, lam_p, sg.reshape(V_DIM, 1), batch, seq, 512, lam_init)
        xp, h2, gates, grow = _mix_prompt(xp, att, u, l, cwp, cb, lg, lb, wo_b, g2, w1, w2, seq, 256)
        xp = _moe_sparse(xp, h2, gates, grow, l, wg_b, wu_b, wd_b, 1024, 256)
        cp_l.append(u.reshape(batch, seq, CONV_CH)[:, seq - CONV_BUF:])

        q, k_all, v_all, u, _ = _inproj(xs, l, depth, kvs, g1, w_in_b, bd, qg, kg, *tabs_s, tm=db)
        kvs = (k_all, v_all)
        q8 = q.astype(F32).reshape(db, N_SUB, HEAD_DIM)
        vn8 = jnp.repeat(v_all[l].reshape(db, N_HEADS, V_DIM), 2, axis=1)
        kn8 = jnp.transpose(k_all[l, 0].reshape(N_SUB, HEAD_DIM, db), (2, 0, 1))
        att = _decode_attention(page_table, q8, jnp.transpose(q8, (0, 2, 1)), kn8, vn8, lam_p, sg, ckt, cvf, l, lam_init)
        att = att.reshape(db, ATTN_WIDTH).astype(BF16)
        xs, h2, gates, ns = _mix_decode(xs, att, u, state_t, l, cwp, cb, lg, lb, wo_b, g2, w1, w2, 64)
        xs = _moe(xs, h2, gates, l, wg_b, wu_b, wd_b, db)
        cs_l.append(jnp.transpose(ns, (1, 0, 2)))

    k_prompt = jnp.transpose(kvp[0].reshape(depth, batch, N_SUB, HEAD_DIM, seq), (0, 1, 4, 2, 3))
    k_sample = jnp.transpose(kvs[0].reshape(depth, 1, N_SUB, HEAD_DIM, db), (0, 4, 1, 2, 3))
    return (xp.reshape(batch, seq, d), xs.reshape(db, 1, d),
            k_prompt, kvp[1].reshape(depth, batch, seq, N_HEADS, V_DIM), jnp.stack(cp_l),
            k_sample, kvs[1].reshape(depth, db, 1, N_HEADS, V_DIM), jnp.stack(cs_l))
```

```python
import functools
import math

import jax
import jax.numpy as jnp
from jax import lax
from jax.experimental import pallas as pl
from jax.experimental.pallas import tpu as pltpu

F32 = jnp.float32
BF16 = jnp.bfloat16

D_MODEL = 1024
HEAD_DIM = 64
N_SUB = 8
N_HEADS = 4
V_DIM = 128
ATTN_WIDTH = 512
CONV_CH = 512
ROT_DIM = 16
ROPE_THETA = 500000.0
CONV_W = 31
CONV_BUF = CONV_W - 1
N_GROUPS = 4
EXP_PER_GROUP = 4
N_EXPERTS = 16
D_EXPERT = 512
EPS = 1e-6
D_IN = 3 * ATTN_WIDTH + 2 * CONV_CH

LANES = 128
SUBLANES = 8
HALO = 32
EXT_TAIL = 16
Q_SCALE = HEAD_DIM ** -0.5 * math.log2(math.e)
QK_AHEAD = 3
GROUP_LANE = N_EXPERTS
VMEM_LIMIT = 56 * 1024 * 1024


def _cparams(n_axes):
    return pltpu.CompilerParams(dimension_semantics=("arbitrary",) * n_axes,
                                vmem_limit_bytes=VMEM_LIMIT)


def _inproj_kernel(x_ref, g1_ref, w_ref, bd_ref, qg_ref, kg_ref, c_ref, s1_ref, s2_ref, *rest):
    q_ref, k_ref, v_ref, u_ref, kb_ref = rest[-5:]
    x = x_ref[...]
    ms = jnp.mean(x * x, axis=-1, keepdims=True)
    h = (x * lax.rsqrt(ms + EPS) * g1_ref[...]).astype(BF16)
    c, s1, s2 = c_ref[...], s1_ref[...], s2_ref[...]
    bd = bd_ref[...]

    def proj(lo, width):
        return jnp.dot(h, w_ref[:, lo:lo + width], preferred_element_type=F32)

    def head_ms(t):
        return jnp.dot((t * t).astype(BF16), bd, preferred_element_type=F32) * (1.0 / HEAD_DIM)

    def norm_rope(t, hms, g):
        tn = t * lax.rsqrt(hms + EPS) * g
        outs = []
        for j in range(ATTN_WIDTH // LANES):
            blk = tn[:, j * LANES:(j + 1) * LANES]
            up = pltpu.roll(blk, LANES - ROT_DIM // 2, 1)
            dn = pltpu.roll(blk, ROT_DIM // 2, 1)
            outs.append(blk * c + up * s1 + dn * s2)
        return outs

    zq = proj(0, ATTN_WIDTH)
    zk = proj(ATTN_WIDTH, ATTN_WIDTH)
    ms_q = head_ms(zq)
    zv = proj(2 * ATTN_WIDTH, ATTN_WIDTH)
    ms_k = head_ms(zk)
    za = proj(3 * ATTN_WIDTH, CONV_CH)
    zg = proj(3 * ATTN_WIDTH + CONV_CH, CONV_CH)
    qs = norm_rope(zq, ms_q, qg_ref[...])
    ks = norm_rope(zk, ms_k, kg_ref[...])
    for j in range(ATTN_WIDTH // LANES):
        sl = slice(j * LANES, (j + 1) * LANES)
        q_ref[:, sl] = (qs[j] * Q_SCALE).astype(BF16)
        k_ref[sl, :] = ks[j].T
        kb_ref[:, sl] = ks[j].astype(BF16)
    tm = x_ref.shape[0]
    for hd in range(N_HEADS):
        v_ref[pl.ds(hd, tm, stride=N_HEADS), :] = zv[:, hd * V_DIM:(hd + 1) * V_DIM]
    u_ref[...] = za * jax.nn.sigmoid(zg)


def _inproj(x, layer, depth, kv_prev, g1, w_in_b, bd, qg, kg, ct, s1t, s2t, tm):
    n = x.shape[0]
    npos = ct.shape[0] // tm
    row = lambda i: (i, 0)
    full = lambda i: (0, 0)
    pos = lambda i: (i % npos, 0)
    lrow = lambda i: (layer, i, 0)
    in_specs = [
        pl.BlockSpec((tm, D_MODEL), row),
        pl.BlockSpec((1, D_MODEL), full),
        pl.BlockSpec((None, D_MODEL, D_IN), lambda i: (layer, 0, 0)),
        pl.BlockSpec((ATTN_WIDTH, ATTN_WIDTH), full),
        pl.BlockSpec((1, ATTN_WIDTH), full),
        pl.BlockSpec((1, ATTN_WIDTH), full),
        pl.BlockSpec((tm, LANES), pos),
        pl.BlockSpec((tm, LANES), pos),
        pl.BlockSpec((tm, LANES), pos),
    ]
    args = [x, g1, w_in_b, bd, qg, kg, ct, s1t, s2t]
    aliases = {}
    if kv_prev is not None:
        aliases = {len(args): 1, len(args) + 1: 2}
        in_specs += [pl.BlockSpec(memory_space=pl.ANY)] * 2
        args += list(kv_prev)
    return pl.pallas_call(
        _inproj_kernel,
        grid=(n // tm,),
        in_specs=in_specs,
        out_specs=[pl.BlockSpec((tm, ATTN_WIDTH), row),
                   pl.BlockSpec((None, None, ATTN_WIDTH, tm), lambda i: (layer, i // npos, 0, i % npos)),
                   pl.BlockSpec((None, tm * N_HEADS, V_DIM), lrow),
                   pl.BlockSpec((tm, CONV_CH), row),
                   pl.BlockSpec((tm, ATTN_WIDTH), row)],
        out_shape=[
            jax.ShapeDtypeStruct((n, ATTN_WIDTH), BF16),
            jax.ShapeDtypeStruct((depth, n // (npos * tm), ATTN_WIDTH, npos * tm), F32),
            jax.ShapeDtypeStruct((depth, n * N_HEADS, V_DIM), F32),
            jax.ShapeDtypeStruct((n, CONV_CH), F32),
            jax.ShapeDtypeStruct((n, ATTN_WIDTH), BF16),
        ],
        input_output_aliases=aliases,
        compiler_params=_cparams(1),
        name="inproj",
    )(*args)


def _lambda_value(lam_ref, lam_init):
    lp = lam_ref[...]
    t1 = jnp.sum(lp[0:1] * lp[1:2], axis=1, keepdims=True)
    t2 = jnp.sum(lp[2:3] * lp[3:4], axis=1, keepdims=True)
    return jnp.exp(t1) - jnp.exp(t2) + lam_init


def _subln(o, g, lam_init):
    ms = jnp.mean(o * o, axis=-1, keepdims=True)
    return o * lax.rsqrt(ms + EPS) * g * (1.0 - lam_init)


def _pattn_kernel(q_ref, k_ref, v_ref, lam_ref, sgc_ref, o_ref, vt_scr, qs_scr, m_scr, l_scr, acc_scr,
                  *, tq, lam_init):
    i = pl.program_id(1)
    nkv = k_ref.shape[0] // tq

    @pl.when(i == 0)
    def _():
        for c in range(nkv):
            for hd in range(N_HEADS):
                rows = v_ref[pl.ds(c * tq * N_HEADS + hd, tq, stride=N_HEADS), :]
                vt_scr[c, hd * V_DIM:(hd + 1) * V_DIM, :] = rows.T.astype(BF16)

    lam = _lambda_value(lam_ref, lam_init)
    lane = lax.broadcasted_iota(jnp.int32, (tq, LANES), 1)
    causal = (lax.broadcasted_iota(jnp.int32, (tq, tq), 0) <= lax.broadcasted_iota(jnp.int32, (tq, tq), 1))

    for sh in range(N_SUB):
        qp = q_ref[:, (sh // 2) * LANES:(sh // 2 + 1) * LANES]
        keep = (lane < HEAD_DIM) if sh % 2 == 0 else (lane >= HEAD_DIM)
        qs_scr[sh] = jnp.where(keep, qp, jnp.zeros_like(qp))
        m_scr[sh] = jnp.full((1, tq), -jnp.inf, F32)
        l_scr[sh] = jnp.zeros((1, tq), F32)
        acc_scr[sh] = jnp.zeros((V_DIM, tq), F32)

    def step(j, masked):
        off = pl.multiple_of(j * tq, tq)

        def scores(sh):
            kj = k_ref[pl.ds(off, tq), (sh // 2) * LANES:(sh // 2 + 1) * LANES]
            return lax.dot_general(kj, qs_scr[sh], (((1,), (1,)), ((), ())), preferred_element_type=F32)

        pending = [scores(sh) for sh in range(QK_AHEAD)]
        for sh in range(N_SUB):
            sl = slice((sh // 2) * LANES, (sh // 2 + 1) * LANES)
            st = pending.pop(0)
            if sh + QK_AHEAD < N_SUB:
                pending.append(scores(sh + QK_AHEAD))
            if masked:
                st = jnp.where(causal, st, -jnp.inf)
            m_prev = m_scr[sh]
            m_new = jnp.maximum(m_prev, jnp.max(st, axis=0, keepdims=True))
            alpha = jnp.exp2(m_prev - m_new)
            p = jnp.exp2(st - m_new)
            l_scr[sh] = alpha * l_scr[sh] + jnp.sum(p, axis=0, keepdims=True)
            acc_scr[sh] = alpha * acc_scr[sh] + jnp.dot(vt_scr[j, sl, :], p.astype(BF16),
                                                        preferred_element_type=F32)
            m_scr[sh] = m_new

    def body(j, carry):
        step(j, False)
        return carry

    lax.fori_loop(0, i, body, 0)
    step(i, True)

    sgc = sgc_ref[...]
    for hd in range(N_HEADS):
        od = acc_scr[2 * hd] / l_scr[2 * hd] - lam * (acc_scr[2 * hd + 1] / l_scr[2 * hd + 1])
        ms = jnp.mean(od * od, axis=0, keepdims=True)
        on = od * lax.rsqrt(ms + EPS) * sgc * (1.0 - lam_init)
        o_ref[:, hd * LANES:(hd + 1) * LANES] = on.T.astype(BF16)


def _prompt_attention(q, kb, v_all, layer, lam_p, sgc, batch, seq, tq, lam_init):
    nq = seq // tq
    n = batch * seq
    kern = functools.partial(_pattn_kernel, tq=tq, lam_init=lam_init)
    return pl.pallas_call(
        kern,
        grid=(batch, nq),
        in_specs=[
            pl.BlockSpec((tq, ATTN_WIDTH), lambda b, i: (b * nq + i, 0)),
            pl.BlockSpec((seq, ATTN_WIDTH), lambda b, i: (b, 0)),
            pl.BlockSpec((None, seq * N_HEADS, V_DIM), lambda b, i: (layer, b, 0)),
            pl.BlockSpec((4, HEAD_DIM), lambda b, i: (0, 0)),
            pl.BlockSpec((V_DIM, 1), lambda b, i: (0, 0)),
        ],
        out_specs=pl.BlockSpec((tq, ATTN_WIDTH), lambda b, i: (b * nq + i, 0)),
        out_shape=jax.ShapeDtypeStruct((n, ATTN_WIDTH), BF16),
        scratch_shapes=[
            pltpu.VMEM((nq, ATTN_WIDTH, tq), BF16),
            pltpu.VMEM((N_SUB, tq, LANES), BF16),
            pltpu.VMEM((N_SUB, 1, tq), F32),
            pltpu.VMEM((N_SUB, 1, tq), F32),
            pltpu.VMEM((N_SUB, V_DIM, tq), F32),
        ],
        compiler_params=_cparams(2),
        name="prompt_attn",
    )(q, kb, v_all, lam_p, sgc)


def _dattn_kernel(pt_ref, q_ref, qt_ref, kn_ref, vn_ref, lam_ref, sg_ref, *rest, n_pages, page, lam_init):
    k_refs = rest[:n_pages]
    v_refs = rest[n_pages:2 * n_pages]
    o_ref = rest[2 * n_pages]
    del pt_ref
    lam = _lambda_value(lam_ref, lam_init)
    qt = qt_ref[...]
    qcols = [jnp.broadcast_to(qt[:, h:h + 1], (HEAD_DIM, page)) for h in range(N_SUB)]
    s = jnp.concatenate(
        [jnp.concatenate([jnp.sum(k_refs[p][h] * qcols[h], axis=0, keepdims=True) for h in range(N_SUB)], axis=0)
         for p in range(n_pages)], axis=1)
    s_self = jnp.sum(q_ref[...] * kn_ref[...], axis=1, keepdims=True)
    m = jnp.maximum(jnp.max(s, axis=1, keepdims=True), s_self)
    pexp = jnp.exp2(s - m)
    pself = jnp.exp2(s_self - m)
    inv = 1.0 / (jnp.sum(pexp, axis=1, keepdims=True) + pself)
    sub = lax.broadcasted_iota(jnp.int32, (N_SUB, 1), 0)
    coef = jnp.where((sub & 1) == 0, 1.0, -lam) * inv
    a8 = pexp * coef
    a8 = (a8 + pltpu.roll(a8, N_SUB - 1, 0)).astype(BF16)
    as8 = jnp.broadcast_to(pself * coef, (N_SUB, V_DIM))
    as8 = as8 + pltpu.roll(as8, N_SUB - 1, 0)
    o_self = as8 * vn_ref[...]
    rows = []
    for hd in range(N_HEADS):
        acc = jnp.zeros((N_SUB, V_DIM), F32)
        for p in range(n_pages):
            v_hd = v_refs[p][pl.ds(hd, page, stride=N_HEADS), :].astype(BF16)
            acc = acc + jnp.dot(a8[:, p * page:(p + 1) * page], v_hd, preferred_element_type=F32)
        rows.append(acc[2 * hd:2 * hd + 1] + o_self[2 * hd:2 * hd + 1])
    o = jnp.concatenate(rows, axis=0)
    o_ref[...] = _subln(o, sg_ref[...], lam_init)


def _decode_attention(page_table, q8, qt, kn8, vn8, lam_p, sg, ckt, cvf, layer, lam_init):
    db, n_pages = page_table.shape
    page = ckt.shape[-1]
    kern = functools.partial(_dattn_kernel, n_pages=n_pages, page=page, lam_init=lam_init)
    const2 = lambda b, pt: (0, 0)

    def kmap(p):
        return lambda b, pt: (layer, pt[b, p], 0, 0, 0)

    def vmap_(p):
        return lambda b, pt: (layer, pt[b, p], 0, 0)

    in_specs = [
        pl.BlockSpec((None, N_SUB, HEAD_DIM), lambda b, pt: (b, 0, 0)),
        pl.BlockSpec((None, HEAD_DIM, N_SUB), lambda b, pt: (b, 0, 0)),
        pl.BlockSpec((None, N_SUB, HEAD_DIM), lambda b, pt: (b, 0, 0)),
        pl.BlockSpec((None, N_SUB, V_DIM), lambda b, pt: (b, 0, 0)),
        pl.BlockSpec((4, HEAD_DIM), const2),
        pl.BlockSpec((1, V_DIM), const2),
    ]
    in_specs += [pl.BlockSpec((None, None, N_SUB, HEAD_DIM, page), kmap(p)) for p in range(n_pages)]
    in_specs += [pl.BlockSpec((None, None, page * N_HEADS, V_DIM), vmap_(p)) for p in range(n_pages)]
    grid_spec = pltpu.PrefetchScalarGridSpec(
        num_scalar_prefetch=1,
        grid=(db,),
        in_specs=in_specs,
        out_specs=pl.BlockSpec((None, N_HEADS, V_DIM), lambda b, pt: (b, 0, 0)),
    )
    return pl.pallas_call(
        kern,
        grid_spec=grid_spec,
        out_shape=jax.ShapeDtypeStruct((db, N_HEADS, V_DIM), F32),
        compiler_params=_cparams(1),
        name="decode_attn",
    )(page_table, q8, qt, kn8, vn8, lam_p, sg, *([ckt] * n_pages), *([cvf] * n_pages))


def _conv_post(y, lg_ref, lb_ref):
    mu = jnp.mean(y, axis=-1, keepdims=True)
    yc = y - mu
    var = jnp.mean(yc * yc, axis=-1, keepdims=True)
    yn = yc * lax.rsqrt(var + EPS) * lg_ref[...] + lb_ref[...]
    return yn * jax.nn.sigmoid(yn)


def _mix_tail(x, att, cnv, wo_ref, g2_ref, w1_ref, w2_ref, xo_ref, h_ref, gate_ref, grow_ref=None):
    y = (x + jnp.dot(att, wo_ref[0:ATTN_WIDTH, :], preferred_element_type=F32)
         + jnp.dot(cnv.astype(BF16), wo_ref[ATTN_WIDTH:, :], preferred_element_type=F32))
    xo_ref[...] = y
    ms = jnp.mean(y * y, axis=-1, keepdims=True)
    h = y * lax.rsqrt(ms + EPS) * g2_ref[...]
    h_hi = h.astype(BF16)
    h_lo = (h - h_hi.astype(F32)).astype(BF16)
    h_ref[...] = h_hi
    r = jnp.dot(h_hi, w1_ref[...], preferred_element_type=F32)
    lg = r + pltpu.roll(r, LANES // 2, 1) + jnp.dot(h_lo, w2_ref[...], preferred_element_type=F32)
    tm = lg.shape[0]
    lane = lax.broadcasted_iota(jnp.int32, (tm, LANES), 1)
    lanef = lane.astype(F32)
    big = float(LANES)
    neg = -jnp.inf
    gmask = lane < N_GROUPS
    gl = jnp.where(gmask, lg, neg)
    gmax = jnp.max(gl, axis=1, keepdims=True)
    gsum = jnp.sum(jnp.where(gmask, jnp.exp(gl - gmax), 0.0), axis=1, keepdims=True)
    g_w = 1.0 / gsum
    g_idx = jnp.min(jnp.where(gl == gmax, lanef, big), axis=1, keepdims=True)
    lo = N_GROUPS + EXP_PER_GROUP * g_idx
    emask = (lanef >= lo) & (lanef < lo + EXP_PER_GROUP)
    ev = jnp.where(emask, lg, neg)
    v1 = jnp.max(ev, axis=1, keepdims=True)
    i1 = jnp.min(jnp.where(ev == v1, lanef, big), axis=1, keepdims=True)
    ev2 = jnp.where(lanef == i1, neg, ev)
    v2 = jnp.max(ev2, axis=1, keepdims=True)
    i2 = jnp.min(jnp.where(ev2 == v2, lanef, big), axis=1, keepdims=True)
    e2 = jnp.exp(v2 - v1)
    den = 1.0 / (1.0 + e2)
    w1 = den * g_w
    w2 = e2 * den * g_w
    gates = jnp.where(lanef == i1, w1, jnp.where(lanef == i2, w2, 0.0))
    gates = pltpu.roll(gates, LANES - N_GROUPS, 1)
    gate_ref[...] = jnp.where(lane == GROUP_LANE, g_idx, gates)
    if grow_ref is not None:
        grow_ref[...] = jnp.transpose(jnp.broadcast_to(g_idx, (tm, LANES)))[0:SUBLANES, :]


def _mix_prompt_kernel(x_ref, att_ref, u_ref, uh_ref, cw_ref, cb_ref, lg_ref, lb_ref, wo_ref, g2_ref,
                       w1_ref, w2_ref, xo_ref, h_ref, gate_ref, grow_ref, ext_scr, *, tm, tiles_per_seq):
    i = pl.program_id(0)
    first = (i % tiles_per_seq) == 0
    ext_scr[0:HALO, :] = jnp.where(first, 0.0, uh_ref[...])
    ext_scr[HALO:HALO + tm, :] = u_ref[...]
    ext_scr[HALO + tm:, :] = jnp.zeros((EXT_TAIL, CONV_CH), F32)
    lead = HALO - CONV_BUF
    acc = jnp.zeros((tm, CONV_CH), F32) + cb_ref[...]
    for r in range(SUBLANES):
        z = None
        for a in range((CONV_W + lead + SUBLANES - 1) // SUBLANES):
            j = SUBLANES * a + r - lead
            if 0 <= j < CONV_W:
                term = cw_ref[j:j + 1, :] * ext_scr[SUBLANES * a:SUBLANES * a + tm + SUBLANES, :]
                z = term if z is None else z + term
        acc = acc + z[r:r + tm]
    cnv = _conv_post(acc, lg_ref, lb_ref)
    _mix_tail(x_ref[...], att_ref[...], cnv, wo_ref, g2_ref, w1_ref, w2_ref, xo_ref, h_ref, gate_ref, grow_ref)


def _mix_decode_kernel(x_ref, att_ref, u_ref, st_ref, cw_ref, cb_ref, lg_ref, lb_ref, wo_ref, g2_ref,
                       w1_ref, w2_ref, xo_ref, h_ref, gate_ref, ns_ref):
    u = u_ref[...]
    acc = cb_ref[...] + cw_ref[CONV_BUF:CONV_BUF + 1, :] * u
    for j in range(CONV_BUF):
        acc = acc + cw_ref[j:j + 1, :] * st_ref[j]
    for j in range(CONV_BUF - 1):
        ns_ref[j] = st_ref[j + 1]
    ns_ref[CONV_BUF - 1] = u
    cnv = _conv_post(acc, lg_ref, lb_ref)
    _mix_tail(x_ref[...], att_ref[...], cnv, wo_ref, g2_ref, w1_ref, w2_ref, xo_ref, h_ref, gate_ref)


def _mix_common_specs(tm, layer):
    row = lambda i: (i, 0)
    full = lambda i: (0, 0)
    tail_in = [
        pl.BlockSpec((HALO, CONV_CH), full),
        pl.BlockSpec((1, CONV_CH), full),
        pl.BlockSpec((1, CONV_CH), full),
        pl.BlockSpec((1, CONV_CH), full),
        pl.BlockSpec((None, D_MODEL, D_MODEL), lambda i: (layer, 0, 0)),
        pl.BlockSpec((1, D_MODEL), full),
        pl.BlockSpec((D_MODEL, LANES), full),
        pl.BlockSpec((D_MODEL, LANES), full),
    ]
    out_specs = [pl.BlockSpec((tm, D_MODEL), row), pl.BlockSpec((tm, D_MODEL), row),
                 pl.BlockSpec((tm, LANES), row)]
    return row, tail_in, out_specs


def _mix_out_shape(n):
    return [jax.ShapeDtypeStruct((n, D_MODEL), F32), jax.ShapeDtypeStruct((n, D_MODEL), BF16),
            jax.ShapeDtypeStruct((n, LANES), F32)]


def _mix_prompt(x, att, u, layer, cwp, cb, lg, lb, wo_b, g2, w1, w2, seq, tm):
    n = x.shape[0]
    row, tail_in, out_specs = _mix_common_specs(tm, layer)
    halo_map = lambda i: (jnp.maximum(i * (tm // HALO) - 1, 0), 0)
    kern = functools.partial(_mix_prompt_kernel, tm=tm, tiles_per_seq=seq // tm)
    return pl.pallas_call(
        kern,
        grid=(n // tm,),
        in_specs=[pl.BlockSpec((tm, D_MODEL), row), pl.BlockSpec((tm, ATTN_WIDTH), row),
                  pl.BlockSpec((tm, CONV_CH), row), pl.BlockSpec((HALO, CONV_CH), halo_map)] + tail_in,
        out_specs=out_specs + [pl.BlockSpec((SUBLANES, tm), lambda i: (0, i))],
        out_shape=_mix_out_shape(n) + [jax.ShapeDtypeStruct((SUBLANES, n), F32)],
        scratch_shapes=[pltpu.VMEM((tm + HALO + EXT_TAIL, CONV_CH), F32)],
        compiler_params=_cparams(1),
        name="mix_prompt",
    )(x, att, u, u, cwp, cb, lg, lb, wo_b, g2, w1, w2)


def _mix_decode(x, att, u, state_t, layer, cwp, cb, lg, lb, wo_b, g2, w1, w2, tm):
    n = x.shape[0]
    row, tail_in, out_specs = _mix_common_specs(tm, layer)
    return pl.pallas_call(
        _mix_decode_kernel,
        grid=(n // tm,),
        in_specs=[pl.BlockSpec((tm, D_MODEL), row), pl.BlockSpec((tm, ATTN_WIDTH), row),
                  pl.BlockSpec((tm, CONV_CH), row),
                  pl.BlockSpec((None, CONV_BUF, tm, CONV_CH), lambda i: (layer, 0, i, 0))] + tail_in,
        out_specs=out_specs + [pl.BlockSpec((CONV_BUF, tm, CONV_CH), lambda i: (0, i, 0))],
        out_shape=_mix_out_shape(n) + [jax.ShapeDtypeStruct((CONV_BUF, n, CONV_CH), F32)],
        compiler_params=_cparams(1),
        name="mix_decode",
    )(x, att, u, state_t, cwp, cb, lg, lb, wo_b, g2, w1, w2)


def _moe_kernel(x_ref, h_ref, gate_ref, wg_ref, wu_ref, wd_ref, o_ref):
    e = pl.program_id(1)

    @pl.when(e == 0)
    def _():
        o_ref[...] = x_ref[...]

    h = h_ref[...]
    hg = jnp.dot(h, wg_ref[...], preferred_element_type=F32)
    hu = jnp.dot(h, wu_ref[...], preferred_element_type=F32)
    gates = gate_ref[...]
    lane = lax.broadcasted_iota(jnp.int32, gates.shape, 1)
    ge = jnp.sum(jnp.where(lane == e, gates, 0.0), axis=1, keepdims=True)
    act = (hg * jax.nn.sigmoid(hg)) * hu * ge
    o_ref[...] += jnp.dot(act.astype(BF16), wd_ref[...], preferred_element_type=F32)


def _moe(x, h, gates, layer, wg_b, wu_b, wd_b, tm):
    n = x.shape[0]
    row = lambda i, e: (i, 0)
    return pl.pallas_call(
        _moe_kernel,
        grid=(n // tm, N_EXPERTS),
        in_specs=[
            pl.BlockSpec((tm, D_MODEL), row),
            pl.BlockSpec((tm, D_MODEL), row),
            pl.BlockSpec((tm, LANES), row),
            pl.BlockSpec((None, None, D_MODEL, D_EXPERT), lambda i, e: (layer, e, 0, 0)),
            pl.BlockSpec((None, None, D_MODEL, D_EXPERT), lambda i, e: (layer, e, 0, 0)),
            pl.BlockSpec((None, None, D_EXPERT, D_MODEL), lambda i, e: (layer, e, 0, 0)),
        ],
        out_specs=pl.BlockSpec((tm, D_MODEL), row),
        out_shape=jax.ShapeDtypeStruct((n, D_MODEL), F32),
        compiler_params=_cparams(2),
        name="moe",
    )(x, h, gates, wg_b, wu_b, wd_b)


def _moe_sparse_kernel(x_ref, h_ref, gate_ref, grow_ref, wg_ref, wu_ref, wd_ref, o_ref, lt_scr, dcol_scr, drow_scr,
                       cnt_scr, *, rp):
    i = pl.program_id(0)
    g = pl.program_id(1)
    tb = h_ref.shape[0]

    @pl.when((i == 0) & (g == 0))
    def _():
        r = lax.broadcasted_iota(jnp.int32, (tb, tb), 0)
        c = lax.broadcasted_iota(jnp.int32, (tb, tb), 1)
        lt_scr[...] = jnp.where(c < r, 1.0, 0.0).astype(BF16)

    @pl.when(g == 0)
    def _():
        o_ref[...] = x_ref[...]

    gf = g.astype(F32)
    gates = gate_ref[...]
    lane = lax.broadcasted_iota(jnp.int32, gates.shape, 1)
    gcol = jnp.sum(jnp.where(lane == GROUP_LANE, gates, 0.0), axis=1, keepdims=True)
    grow = grow_ref[0:1, :]

    @pl.when(g == 0)
    def _():
        lt = lt_scr[...]
        oh_c = jnp.where(gcol == lane.astype(F32), 1.0, 0.0)
        rk_c = jnp.dot(lt, oh_c.astype(BF16), preferred_element_type=F32)
        dcol_scr[...] = jnp.sum(oh_c * rk_c, axis=1, keepdims=True)
        sub = lax.broadcasted_iota(jnp.int32, (SUBLANES, tb), 0).astype(F32)
        oh_r = jnp.where(grow == sub, 1.0, 0.0)
        rk_r = lax.dot_general(oh_r.astype(BF16), lt, (((1,), (1,)), ((), ())), preferred_element_type=F32)
        drow_scr[...] = jnp.broadcast_to(jnp.sum(oh_r * rk_r, axis=0, keepdims=True), (SUBLANES, tb))
        for l in range(N_GROUPS):
            cnt_scr[l] = jnp.sum(oh_r[l:l + 1, :]).astype(jnp.int32)

    dcol = jnp.where(gcol == gf, dcol_scr[...], -1.0)
    drow = jnp.where(grow == gf, drow_scr[0:1, :], -1.0)
    cnt = cnt_scr[g]
    half = rp // 2
    rem = cnt % rp
    nbig = cnt // rp + (rem > half).astype(jnp.int32)

    g_hi = gates.astype(BF16)
    g_lo = (gates - g_hi.astype(F32)).astype(BF16)
    h = h_ref[...]

    def chunk(first_slot, rows):
        base = first_slot.astype(F32)
        lane_r = lax.broadcasted_iota(jnp.int32, (rows, LANES), 1)
        riota = lax.broadcasted_iota(jnp.int32, (rows, 1), 0).astype(F32)
        ciota = lax.broadcasted_iota(jnp.int32, (1, rows), 1).astype(F32)
        pk = jnp.where(drow == riota + base, 1.0, 0.0).astype(BF16)
        xs = jnp.dot(pk, h, preferred_element_type=F32).astype(BF16)
        gs = (jnp.dot(pk, g_hi, preferred_element_type=F32)
              + jnp.dot(pk, g_lo, preferred_element_type=F32))
        y = jnp.zeros((rows, o_ref.shape[1]), F32)
        for e in range(EXP_PER_GROUP):
            hg = jnp.dot(xs, wg_ref[e], preferred_element_type=F32)
            hu = jnp.dot(xs, wu_ref[e], preferred_element_type=F32)
            ge = jnp.sum(jnp.where(lane_r == g * EXP_PER_GROUP + e, gs, 0.0), axis=1, keepdims=True)
            act = (hg * jax.nn.sigmoid(hg)) * hu * ge
            y = y + jnp.dot(act.astype(BF16), wd_ref[e], preferred_element_type=F32)
        ptk = jnp.where(dcol == ciota + base, 1.0, 0.0).astype(BF16)
        o_ref[...] += jnp.dot(ptk, y.astype(BF16), preferred_element_type=F32)

    def big(k, carry):
        chunk(k * rp, rp)
        return carry

    lax.fori_loop(0, nbig, big, 0)

    @pl.when((rem > 0) & (rem <= half))
    def _():
        chunk(nbig * rp, half)


def _moe_sparse(x, h, gates, grow, layer, wg_b, wu_b, wd_b, tb, rp):
    n, d = x.shape
    de = wg_b.shape[-1]
    row = lambda i, g: (i, 0)
    kern = functools.partial(_moe_sparse_kernel, rp=rp)
    return pl.pallas_call(
        kern,
        grid=(n // tb, N_GROUPS),
        in_specs=[
            pl.BlockSpec((tb, d), row),
            pl.BlockSpec((tb, d), row),
            pl.BlockSpec((tb, LANES), row),
            pl.BlockSpec((SUBLANES, tb), lambda i, g: (0, i)),
            pl.BlockSpec((None, EXP_PER_GROUP, d, de), lambda i, g: (layer, g, 0, 0)),
            pl.BlockSpec((None, EXP_PER_GROUP, d, de), lambda i, g: (layer, g, 0, 0)),
            pl.BlockSpec((None, EXP_PER_GROUP, de, d), lambda i, g: (layer, g, 0, 0)),
        ],
        out_specs=pl.BlockSpec((tb, d), row),
        out_shape=jax.ShapeDtypeStruct((n, d), F32),
        scratch_shapes=[pltpu.VMEM((tb, tb), BF16), pltpu.VMEM((tb, 1), F32), pltpu.VMEM((SUBLANES, tb), F32),
                        pltpu.SMEM((N_GROUPS,), jnp.int32)],
        compiler_params=_cparams(2),
        name="moe_sparse",
    )(x, h, gates, grow, wg_b, wu_b, wd_b)


def _rope_tables(pos):
    half = ROT_DIM // 2
    inv = jnp.power(ROPE_THETA, -jnp.arange(0, ROT_DIM, 2, dtype=F32) / ROT_DIM)
    ang = pos.astype(F32)[:, None] * inv[None, :]
    cos, sin = jnp.cos(ang), jnp.sin(ang)
    n = pos.shape[0]
    pad = jnp.zeros((n, HEAD_DIM - ROT_DIM), F32)
    zer = jnp.zeros((n, half), F32)
    c = jnp.concatenate([cos, cos, pad + 1.0], axis=1)
    s1 = jnp.concatenate([-sin, zer, pad], axis=1)
    s2 = jnp.concatenate([zer, sin, pad], axis=1)
    rep = LANES // HEAD_DIM
    return jnp.tile(c, (1, rep)), jnp.tile(s1, (1, rep)), jnp.tile(s2, (1, rep))


def kernel(x_prompt, x_sample, cache_k, cache_v, state_conv, page_table, norm1_g, w_in, q_norm_g, k_norm_g,
           lam_q1, lam_k1, lam_q2, lam_k2, subln_g, conv_w, conv_b, conv_ln_g, conv_ln_b, w_out, norm2_g,
           w_router_group, w_router_expert, w_gate, w_up, w_down):
    batch, seq, d = x_prompt.shape
    db = x_sample.shape[0]
    depth = w_in.shape[0]
    n_pages, page = page_table.shape[1], cache_k.shape[2]
    past = n_pages * page
    n = batch * seq

    xp = x_prompt.reshape(n, d)
    xs = x_sample.reshape(db, d)
    tabs_p = _rope_tables(jnp.arange(seq, dtype=jnp.int32))
    tabs_s = _rope_tables(jnp.full((db,), past, dtype=jnp.int32))
    gi = jnp.arange(ATTN_WIDTH, dtype=jnp.int32) // HEAD_DIM
    bd = (gi[:, None] == gi[None, :]).astype(BF16)
    ckt = jnp.transpose(cache_k, (0, 1, 3, 4, 2))
    cvf = cache_v.reshape(depth, cache_v.shape[1], page * N_HEADS, V_DIM)
    state_t = jnp.transpose(state_conv, (0, 2, 1, 3))
    w_in_b, wo_b = w_in.astype(BF16), w_out.astype(BF16)
    wg_b, wu_b, wd_b = w_gate.astype(BF16), w_up.astype(BF16), w_down.astype(BF16)
    kvp = kvs = None

    cp_l, cs_l = [], []
    for l in range(depth):
        lam_init = 0.8 - 0.6 * math.exp(-0.3 * l)
        g1 = norm1_g[l][None, :]
        qg = jnp.tile(q_norm_g[l], N_SUB)[None, :]
        kg = jnp.tile(k_norm_g[l], N_SUB)[None, :]
        lam_p = jnp.stack([lam_q1[l], lam_k1[l], lam_q2[l], lam_k2[l]])
        sg = subln_g[l][None, :]
        cwp = jnp.pad(conv_w[l], ((0, HALO - CONV_W), (0, 0)))
        cb, lg, lb = conv_b[l][None, :], conv_ln_g[l][None, :], conv_ln_b[l][None, :]
        g2 = norm2_g[l][None, :]
        wr = jnp.concatenate([w_router_group[l], w_router_expert[l]], axis=1)
        wr_hi = wr.astype(BF16)
        wr_lo = (wr - wr_hi.astype(F32)).astype(BF16)
        nr = wr.shape[1]
        w1 = jnp.zeros((d, LANES), BF16).at[:, :nr].set(wr_hi).at[:, LANES // 2:LANES // 2 + nr].set(wr_lo)
        w2 = jnp.zeros((d, LANES), BF16).at[:, :nr].set(wr_hi)

        q, k_all, v_all, u, kb = _inproj(xp, l, depth, kvp, g1, w_in_b, bd, qg, kg, *tabs_p, tm=512)
        kvp = (k_all, v_all)
        att = _prompt_attention(q, kb, v_all, l, lam_p, sg.reshape(V_DIM, 1), batch, seq, 512, lam_init)
        xp, h2, gates, grow = _mix_prompt(xp, att, u, l, cwp, cb, lg, lb, wo_b, g2, w1, w2, seq, 256)
        xp = _moe_sparse(xp, h2, gates, grow, l, wg_b, wu_b, wd_b, 1024, 256)
        cp_l.append(u.reshape(batch, seq, CONV_CH)[:, seq - CONV_BUF:])

        q, k_all, v_all, u, _ = _inproj(xs, l, depth, kvs, g1, w_in_b, bd, qg, kg, *tabs_s, tm=db)
        kvs = (k_all, v_all)
        q8 = q.astype(F32).reshape(db, N_SUB, HEAD_DIM)
        vn8 = jnp.repeat(v_all[l].reshape(db, N_HEADS, V_DIM), 2, axis=1)
        kn8 = jnp.transpose(k_all[l, 0].reshape(N_SUB, HEAD_DIM, db), (2, 0, 1))
        att = _decode_attention(page_table, q8, jnp.transpose(q8, (0, 2, 1)), kn8, vn8, lam_p, sg, ckt, cvf, l, lam_init)
        att = att.reshape(db, ATTN_WIDTH).astype(BF16)
        xs, h2, gates, ns = _mix_decode(xs, att, u, state_t, l, cwp, cb, lg, lb, wo_b, g2, w1, w2, 64)
        xs = _moe(xs, h2, gates, l, wg_b, wu_b, wd_b, db)
        cs_l.append(jnp.transpose(ns, (1, 0, 2)))

    k_prompt = jnp.transpose(kvp[0].reshape(depth, batch, N_SUB, HEAD_DIM, seq), (0, 1, 4, 2, 3))
    k_sample = jnp.transpose(kvs[0].reshape(depth, 1, N_SUB, HEAD_DIM, db), (0, 4, 1, 2, 3))
    return (xp.reshape(batch, seq, d), xs.reshape(db, 1, d),
            k_prompt, kvp[1].reshape(depth, batch, seq, N_HEADS, V_DIM), jnp.stack(cp_l),
            k_sample, kvs[1].reshape(depth, db, 1, N_HEADS, V_DIM), jnp.stack(cs_l))
```

```python
import functools
import math

import jax
import jax.numpy as jnp
from jax import lax
from jax.experimental import pallas as pl
from jax.experimental.pallas import tpu as pltpu

F32 = jnp.float32
BF16 = jnp.bfloat16

D_MODEL = 1024
HEAD_DIM = 64
N_SUB = 8
N_HEADS = 4
V_DIM = 128
ATTN_WIDTH = 512
CONV_CH = 512
ROT_DIM = 16
ROPE_THETA = 500000.0
CONV_W = 31
CONV_BUF = CONV_W - 1
N_GROUPS = 4
EXP_PER_GROUP = 4
N_EXPERTS = 16
D_EXPERT = 512
EPS = 1e-6
D_IN = 3 * ATTN_WIDTH + 2 * CONV_CH

LANES = 128
SUBLANES = 8
HALO = 32
EXT_TAIL = 16
Q_SCALE = HEAD_DIM ** -0.5 * math.log2(math.e)
QK_AHEAD = 4
GROUP_LANE = N_EXPERTS
VMEM_LIMIT = 56 * 1024 * 1024


def _cparams(n_axes):
    return pltpu.CompilerParams(dimension_semantics=("arbitrary",) * n_axes,
                                vmem_limit_bytes=VMEM_LIMIT)


def _inproj_kernel(x_ref, g1_ref, w_ref, bd_ref, qg_ref, kg_ref, c_ref, s1_ref, s2_ref, *rest):
    q_ref, k_ref, v_ref, u_ref, kb_ref = rest[-5:]
    x = x_ref[...]
    ms = jnp.mean(x * x, axis=-1, keepdims=True)
    h = (x * lax.rsqrt(ms + EPS) * g1_ref[...]).astype(BF16)
    c, s1, s2 = c_ref[...], s1_ref[...], s2_ref[...]
    bd = bd_ref[...]

    def proj(lo, width):
        return jnp.dot(h, w_ref[:, lo:lo + width], preferred_element_type=F32)

    def head_ms(t):
        return jnp.dot((t * t).astype(BF16), bd, preferred_element_type=F32) * (1.0 / HEAD_DIM)

    def norm_rope(t, hms, g):
        tn = t * lax.rsqrt(hms + EPS) * g
        outs = []
        for j in range(ATTN_WIDTH // LANES):
            blk = tn[:, j * LANES:(j + 1) * LANES]
            up = pltpu.roll(blk, LANES - ROT_DIM // 2, 1)
            dn = pltpu.roll(blk, ROT_DIM // 2, 1)
            outs.append(blk * c + up * s1 + dn * s2)
        return outs

    zq = proj(0, ATTN_WIDTH)
    zk = proj(ATTN_WIDTH, ATTN_WIDTH)
    ms_q = head_ms(zq)
    zv = proj(2 * ATTN_WIDTH, ATTN_WIDTH)
    ms_k = head_ms(zk)
    za = proj(3 * ATTN_WIDTH, CONV_CH)
    zg = proj(3 * ATTN_WIDTH + CONV_CH, CONV_CH)
    qs = norm_rope(zq, ms_q, qg_ref[...])
    ks = norm_rope(zk, ms_k, kg_ref[...])
    for j in range(ATTN_WIDTH // LANES):
        sl = slice(j * LANES, (j + 1) * LANES)
        q_ref[:, sl] = (qs[j] * Q_SCALE).astype(BF16)
        k_ref[sl, :] = ks[j].T
        kb_ref[:, sl] = ks[j].astype(BF16)
    tm = x_ref.shape[0]
    for hd in range(N_HEADS):
        v_ref[pl.ds(hd, tm, stride=N_HEADS), :] = zv[:, hd * V_DIM:(hd + 1) * V_DIM]
    u_ref[...] = za * jax.nn.sigmoid(zg)


def _inproj(x, layer, depth, kv_prev, g1, w_in_b, bd, qg, kg, ct, s1t, s2t, tm):
    n = x.shape[0]
    npos = ct.shape[0] // tm
    row = lambda i: (i, 0)
    full = lambda i: (0, 0)
    pos = lambda i: (i % npos, 0)
    lrow = lambda i: (layer, i, 0)
    in_specs = [
        pl.BlockSpec((tm, D_MODEL), row),
        pl.BlockSpec((1, D_MODEL), full),
        pl.BlockSpec((None, D_MODEL, D_IN), lambda i: (layer, 0, 0)),
        pl.BlockSpec((ATTN_WIDTH, ATTN_WIDTH), full),
        pl.BlockSpec((1, ATTN_WIDTH), full),
        pl.BlockSpec((1, ATTN_WIDTH), full),
        pl.BlockSpec((tm, LANES), pos),
        pl.BlockSpec((tm, LANES), pos),
        pl.BlockSpec((tm, LANES), pos),
    ]
    args = [x, g1, w_in_b, bd, qg, kg, ct, s1t, s2t]
    aliases = {}
    if kv_prev is not None:
        aliases = {len(args): 1, len(args) + 1: 2}
        in_specs += [pl.BlockSpec(memory_space=pl.ANY)] * 2
        args += list(kv_prev)
    return pl.pallas_call(
        _inproj_kernel,
        grid=(n // tm,),
        in_specs=in_specs,
        out_specs=[pl.BlockSpec((tm, ATTN_WIDTH), row),
                   pl.BlockSpec((None, None, ATTN_WIDTH, tm), lambda i: (layer, i // npos, 0, i % npos)),
                   pl.BlockSpec((None, tm * N_HEADS, V_DIM), lrow),
                   pl.BlockSpec((tm, CONV_CH), row),
                   pl.BlockSpec((tm, ATTN_WIDTH), row)],
        out_shape=[
            jax.ShapeDtypeStruct((n, ATTN_WIDTH), BF16),
            jax.ShapeDtypeStruct((depth, n // (npos * tm), ATTN_WIDTH, npos * tm), F32),
            jax.ShapeDtypeStruct((depth, n * N_HEADS, V_DIM), F32),
            jax.ShapeDtypeStruct((n, CONV_CH), F32),
            jax.ShapeDtypeStruct((n, ATTN_WIDTH), BF16),
        ],
        input_output_aliases=aliases,
        compiler_params=_cparams(1),
        name="inproj",
    )(*args)


def _lambda_value(lam_ref, lam_init):
    lp = lam_ref[...]
    t1 = jnp.sum(lp[0:1] * lp[1:2], axis=1, keepdims=True)
    t2 = jnp.sum(lp[2:3] * lp[3:4], axis=1, keepdims=True)
    return jnp.exp(t1) - jnp.exp(t2) + lam_init


def _subln(o, g, lam_init):
    ms = jnp.mean(o * o, axis=-1, keepdims=True)
    return o * lax.rsqrt(ms + EPS) * g * (1.0 - lam_init)


def _pattn_kernel(q_ref, k_ref, v_ref, lam_ref, sgc_ref, o_ref, vt_scr, qs_scr, m_scr, l_scr, acc_scr,
                  *, tq, lam_init):
    i = pl.program_id(1)
    nkv = k_ref.shape[0] // tq

    @pl.when(i == 0)
    def _():
        for c in range(nkv):
            for hd in range(N_HEADS):
                rows = v_ref[pl.ds(c * tq * N_HEADS + hd, tq, stride=N_HEADS), :]
                vt_scr[c, hd * V_DIM:(hd + 1) * V_DIM, :] = rows.T.astype(BF16)

    lam = _lambda_value(lam_ref, lam_init)
    lane = lax.broadcasted_iota(jnp.int32, (tq, LANES), 1)
    causal = (lax.broadcasted_iota(jnp.int32, (tq, tq), 0) <= lax.broadcasted_iota(jnp.int32, (tq, tq), 1))

    for sh in range(N_SUB):
        qp = q_ref[:, (sh // 2) * LANES:(sh // 2 + 1) * LANES]
        keep = (lane < HEAD_DIM) if sh % 2 == 0 else (lane >= HEAD_DIM)
        qs_scr[sh] = jnp.where(keep, qp, jnp.zeros_like(qp))
        m_scr[sh] = jnp.full((1, tq), -jnp.inf, F32)
        l_scr[sh] = jnp.zeros((1, tq), F32)
        acc_scr[sh] = jnp.zeros((V_DIM, tq), F32)

    def step(j, masked):
        off = pl.multiple_of(j * tq, tq)
        hq = tq // 2
        chains = [(sh, qh) for sh in range(N_SUB) for qh in range(2)]

        def nkeys(qh):
            return hq if (masked and qh == 0) else tq

        def scores(sh, qh):
            kj = k_ref[pl.ds(off, nkeys(qh)), (sh // 2) * LANES:(sh // 2 + 1) * LANES]
            return lax.dot_general(kj, qs_scr[sh, qh * hq:(qh + 1) * hq, :], (((1,), (1,)), ((), ())),
                                   preferred_element_type=F32)

        pending = [scores(*chains[c]) for c in range(QK_AHEAD)]
        for c, (sh, qh) in enumerate(chains):
            sl = slice((sh // 2) * LANES, (sh // 2 + 1) * LANES)
            cols = slice(qh * hq, (qh + 1) * hq)
            nk = nkeys(qh)
            st = pending.pop(0)
            if c + QK_AHEAD < len(chains):
                pending.append(scores(*chains[c + QK_AHEAD]))
            if masked:
                st = jnp.where(causal[0:nk, cols], st, -jnp.inf)
            m_prev = m_scr[sh, :, cols]
            m_new = jnp.maximum(m_prev, jnp.max(st, axis=0, keepdims=True))
            alpha = jnp.exp2(m_prev - m_new)
            p = jnp.exp2(st - m_new)
            l_scr[sh, :, cols] = alpha * l_scr[sh, :, cols] + jnp.sum(p, axis=0, keepdims=True)
            acc_scr[sh, :, cols] = alpha * acc_scr[sh, :, cols] + jnp.dot(vt_scr[j, sl, 0:nk], p.astype(BF16),
                                                                          preferred_element_type=F32)
            m_scr[sh, :, cols] = m_new

    def body(j, carry):
        step(j, False)
        return carry

    lax.fori_loop(0, i, body, 0)
    step(i, True)

    sgc = sgc_ref[...]
    for hd in range(N_HEADS):
        od = acc_scr[2 * hd] / l_scr[2 * hd] - lam * (acc_scr[2 * hd + 1] / l_scr[2 * hd + 1])
        ms = jnp.mean(od * od, axis=0, keepdims=True)
        on = od * lax.rsqrt(ms + EPS) * sgc * (1.0 - lam_init)
        o_ref[:, hd * LANES:(hd + 1) * LANES] = on.T.astype(BF16)


def _prompt_attention(q, kb, v_all, layer, lam_p, sgc, batch, seq, tq, lam_init):
    nq = seq // tq
    n = batch * seq
    kern = functools.partial(_pattn_kernel, tq=tq, lam_init=lam_init)
    return pl.pallas_call(
        kern,
        grid=(batch, nq),
        in_specs=[
            pl.BlockSpec((tq, ATTN_WIDTH), lambda b, i: (b * nq + i, 0)),
            pl.BlockSpec((seq, ATTN_WIDTH), lambda b, i: (b, 0)),
            pl.BlockSpec((None, seq * N_HEADS, V_DIM), lambda b, i: (layer, b, 0)),
            pl.BlockSpec((4, HEAD_DIM), lambda b, i: (0, 0)),
            pl.BlockSpec((V_DIM, 1), lambda b, i: (0, 0)),
        ],
        out_specs=pl.BlockSpec((tq, ATTN_WIDTH), lambda b, i: (b * nq + i, 0)),
        out_shape=jax.ShapeDtypeStruct((n, ATTN_WIDTH), BF16),
        scratch_shapes=[
            pltpu.VMEM((nq, ATTN_WIDTH, tq), BF16),
            pltpu.VMEM((N_SUB, tq, LANES), BF16),
            pltpu.VMEM((N_SUB, 1, tq), F32),
            pltpu.VMEM((N_SUB, 1, tq), F32),
            pltpu.VMEM((N_SUB, V_DIM, tq), F32),
        ],
        compiler_params=_cparams(2),
        name="prompt_attn",
    )(q, kb, v_all, lam_p, sgc)


def _decode_attend(q8, qt, kn, vn, lam, sg, k_refs, v_refs, page, lam_init):
    n_pages = len(k_refs)
    qcols = [jnp.broadcast_to(qt[:, h:h + 1], (HEAD_DIM, page)) for h in range(N_SUB)]
    s = jnp.concatenate(
        [jnp.concatenate([jnp.sum(k_refs[p][h] * qcols[h], axis=0, keepdims=True) for h in range(N_SUB)], axis=0)
         for p in range(n_pages)], axis=1)
    s_self = jnp.sum(q8 * kn, axis=1, keepdims=True)
    m = jnp.maximum(jnp.max(s, axis=1, keepdims=True), s_self)
    pexp = jnp.exp2(s - m)
    pself = jnp.exp2(s_self - m)
    inv = 1.0 / (jnp.sum(pexp, axis=1, keepdims=True) + pself)
    sub = lax.broadcasted_iota(jnp.int32, (N_SUB, 1), 0)
    coef = jnp.where((sub & 1) == 0, 1.0, -lam) * inv
    a8 = pexp * coef
    a8 = (a8 + pltpu.roll(a8, N_SUB - 1, 0)).astype(BF16)
    as8 = jnp.broadcast_to(pself * coef, (N_SUB, V_DIM))
    as8 = as8 + pltpu.roll(as8, N_SUB - 1, 0)
    o_self = as8 * vn
    rows = []
    for hd in range(N_HEADS):
        v_hd = jnp.concatenate([v_refs[p][pl.ds(hd, page, stride=N_HEADS), :].astype(BF16) for p in range(n_pages)],
                               axis=0)
        acc = jnp.dot(a8, v_hd, preferred_element_type=F32)
        rows.append(acc[2 * hd:2 * hd + 1] + o_self[2 * hd:2 * hd + 1])
    o = jnp.concatenate(rows, axis=0)
    return _subln(o, sg, lam_init)


def _conv_post(y, lg_ref, lb_ref):
    mu = jnp.mean(y, axis=-1, keepdims=True)
    yc = y - mu
    var = jnp.mean(yc * yc, axis=-1, keepdims=True)
    yn = yc * lax.rsqrt(var + EPS) * lg_ref[...] + lb_ref[...]
    return yn * jax.nn.sigmoid(yn)


def _mix_tail(x, att, cnv, wo_ref, g2_ref, w1_ref, w2_ref, xo_ref, h_ref, gate_ref, grow_ref=None):
    y = (x + jnp.dot(att, wo_ref[0:ATTN_WIDTH, :], preferred_element_type=F32)
         + jnp.dot(cnv.astype(BF16), wo_ref[ATTN_WIDTH:, :], preferred_element_type=F32))
    xo_ref[...] = y
    ms = jnp.mean(y * y, axis=-1, keepdims=True)
    h = y * lax.rsqrt(ms + EPS) * g2_ref[...]
    h_hi = h.astype(BF16)
    h_lo = (h - h_hi.astype(F32)).astype(BF16)
    h_ref[...] = h_hi
    r = jnp.dot(h_hi, w1_ref[...], preferred_element_type=F32)
    lg = r + pltpu.roll(r, LANES // 2, 1) + jnp.dot(h_lo, w2_ref[...], preferred_element_type=F32)
    tm = lg.shape[0]
    lane = lax.broadcasted_iota(jnp.int32, (tm, LANES), 1)
    lanef = lane.astype(F32)
    big = float(LANES)
    neg = -jnp.inf
    gmask = lane < N_GROUPS
    gl = jnp.where(gmask, lg, neg)
    gmax = jnp.max(gl, axis=1, keepdims=True)
    gsum = jnp.sum(jnp.where(gmask, jnp.exp(gl - gmax), 0.0), axis=1, keepdims=True)
    g_w = 1.0 / gsum
    g_idx = jnp.min(jnp.where(gl == gmax, lanef, big), axis=1, keepdims=True)
    lo = N_GROUPS + EXP_PER_GROUP * g_idx
    emask = (lanef >= lo) & (lanef < lo + EXP_PER_GROUP)
    ev = jnp.where(emask, lg, neg)
    v1 = jnp.max(ev, axis=1, keepdims=True)
    i1 = jnp.min(jnp.where(ev == v1, lanef, big), axis=1, keepdims=True)
    ev2 = jnp.where(lanef == i1, neg, ev)
    v2 = jnp.max(ev2, axis=1, keepdims=True)
    i2 = jnp.min(jnp.where(ev2 == v2, lanef, big), axis=1, keepdims=True)
    e2 = jnp.exp(v2 - v1)
    den = 1.0 / (1.0 + e2)
    w1 = den * g_w
    w2 = e2 * den * g_w
    gates = jnp.where(lanef == i1, w1, jnp.where(lanef == i2, w2, 0.0))
    gates = pltpu.roll(gates, LANES - N_GROUPS, 1)
    gate_ref[...] = jnp.where(lane == GROUP_LANE, g_idx, gates)
    if grow_ref is not None:
        grow_ref[...] = jnp.transpose(jnp.broadcast_to(g_idx, (tm, LANES)))[0:SUBLANES, :]


def _mix_prompt_kernel(pt_ref, x_ref, att_ref, u_ref, uh_ref, cw_ref, cb_ref, lg_ref, lb_ref, wo_ref, g2_ref,
                       w1_ref, w2_ref, q8_ref, qt_ref, kn_ref, vn_ref, lam_ref, sg_ref, *rest,
                       tm, tiles_per_seq, n_pages, page, per_step, lam_init):
    del pt_ref
    nkv = per_step * n_pages
    k_refs, v_refs = rest[:nkv], rest[nkv:2 * nkv]
    xo_ref, h_ref, gate_ref, grow_ref, od_ref, ext_scr = rest[2 * nkv:]
    i = pl.program_id(0)
    first = (i % tiles_per_seq) == 0
    ext_scr[0:HALO, :] = jnp.where(first, 0.0, uh_ref[...])
    ext_scr[HALO:HALO + tm, :] = u_ref[...]
    ext_scr[HALO + tm:, :] = jnp.zeros((EXT_TAIL, CONV_CH), F32)
    lead = HALO - CONV_BUF
    acc = jnp.zeros((tm, CONV_CH), F32) + cb_ref[...]
    for r in range(SUBLANES):
        z = None
        for a in range((CONV_W + lead + SUBLANES - 1) // SUBLANES):
            j = SUBLANES * a + r - lead
            if 0 <= j < CONV_W:
                term = cw_ref[j:j + 1, :] * ext_scr[SUBLANES * a:SUBLANES * a + tm + SUBLANES, :]
                z = term if z is None else z + term
        acc = acc + z[r:r + tm]
    cnv = _conv_post(acc, lg_ref, lb_ref)
    _mix_tail(x_ref[...], att_ref[...], cnv, wo_ref, g2_ref, w1_ref, w2_ref, xo_ref, h_ref, gate_ref, grow_ref)
    lam = _lambda_value(lam_ref, lam_init)
    for sq in range(per_step):
        od_ref[sq] = _decode_attend(q8_ref[sq], qt_ref[sq], kn_ref[sq], vn_ref[sq], lam, sg_ref[...],
                                    k_refs[sq * n_pages:(sq + 1) * n_pages], v_refs[sq * n_pages:(sq + 1) * n_pages],
                                    page, lam_init)


def _mix_decode_kernel(x_ref, att_ref, u_ref, st_ref, cw_ref, cb_ref, lg_ref, lb_ref, wo_ref, g2_ref,
                       w1_ref, w2_ref, xo_ref, h_ref, gate_ref, ns_ref):
    u = u_ref[...]
    acc = cb_ref[...] + cw_ref[CONV_BUF:CONV_BUF + 1, :] * u
    for j in range(CONV_BUF):
        acc = acc + cw_ref[j:j + 1, :] * st_ref[j]
    for j in range(CONV_BUF - 1):
        ns_ref[j] = st_ref[j + 1]
    ns_ref[CONV_BUF - 1] = u
    cnv = _conv_post(acc, lg_ref, lb_ref)
    _mix_tail(x_ref[...], att_ref[...], cnv, wo_ref, g2_ref, w1_ref, w2_ref, xo_ref, h_ref, gate_ref)


def _mix_common_specs(tm, layer):
    row = lambda i, *_: (i, 0)
    full = lambda i, *_: (0, 0)
    tail_in = [
        pl.BlockSpec((HALO, CONV_CH), full),
        pl.BlockSpec((1, CONV_CH), full),
        pl.BlockSpec((1, CONV_CH), full),
        pl.BlockSpec((1, CONV_CH), full),
        pl.BlockSpec((None, D_MODEL, D_MODEL), lambda i, *_: (layer, 0, 0)),
        pl.BlockSpec((1, D_MODEL), full),
        pl.BlockSpec((D_MODEL, LANES), full),
        pl.BlockSpec((D_MODEL, LANES), full),
    ]
    out_specs = [pl.BlockSpec((tm, D_MODEL), row), pl.BlockSpec((tm, D_MODEL), row),
                 pl.BlockSpec((tm, LANES), row)]
    return row, tail_in, out_specs


def _mix_out_shape(n):
    return [jax.ShapeDtypeStruct((n, D_MODEL), F32), jax.ShapeDtypeStruct((n, D_MODEL), BF16),
            jax.ShapeDtypeStruct((n, LANES), F32)]


def _mix_prompt(x, att, u, layer, cwp, cb, lg, lb, wo_b, g2, w1, w2, seq, tm,
                page_table, q8, qt, kn8, vn8, lam_p, sg, ckt, cvf, lam_init):
    n = x.shape[0]
    steps = n // tm
    db, n_pages = page_table.shape
    page = ckt.shape[-1]
    per_step = db // steps
    assert per_step * steps == db
    row, tail_in, out_specs = _mix_common_specs(tm, layer)
    halo_map = lambda i, pt: (jnp.maximum(i * (tm // HALO) - 1, 0), 0)
    seq3 = lambda i, pt: (i, 0, 0)
    const2 = lambda i, pt: (0, 0)

    def kmap(sq, p):
        return lambda i, pt: (layer, pt[i * per_step + sq, p], 0, 0, 0)

    def vmap_(sq, p):
        return lambda i, pt: (layer, pt[i * per_step + sq, p], 0, 0)

    in_specs = [pl.BlockSpec((tm, D_MODEL), row), pl.BlockSpec((tm, ATTN_WIDTH), row),
                pl.BlockSpec((tm, CONV_CH), row), pl.BlockSpec((HALO, CONV_CH), halo_map)] + tail_in
    in_specs += [
        pl.BlockSpec((per_step, N_SUB, HEAD_DIM), seq3),
        pl.BlockSpec((per_step, HEAD_DIM, N_SUB), seq3),
        pl.BlockSpec((per_step, N_SUB, HEAD_DIM), seq3),
        pl.BlockSpec((per_step, N_SUB, V_DIM), seq3),
        pl.BlockSpec((4, HEAD_DIM), const2),
        pl.BlockSpec((1, V_DIM), const2),
    ]
    in_specs += [pl.BlockSpec((None, None, N_SUB, HEAD_DIM, page), kmap(sq, p))
                 for sq in range(per_step) for p in range(n_pages)]
    in_specs += [pl.BlockSpec((None, None, page * N_HEADS, V_DIM), vmap_(sq, p))
                 for sq in range(per_step) for p in range(n_pages)]
    kern = functools.partial(_mix_prompt_kernel, tm=tm, tiles_per_seq=seq // tm, n_pages=n_pages, page=page,
                             per_step=per_step, lam_init=lam_init)
    grid_spec = pltpu.PrefetchScalarGridSpec(
        num_scalar_prefetch=1,
        grid=(steps,),
        in_specs=in_specs,
        out_specs=out_specs + [pl.BlockSpec((SUBLANES, tm), lambda i, pt: (0, i)),
                               pl.BlockSpec((per_step, N_HEADS, V_DIM), seq3)],
        scratch_shapes=[pltpu.VMEM((tm + HALO + EXT_TAIL, CONV_CH), F32)],
    )
    return pl.pallas_call(
        kern,
        grid_spec=grid_spec,
        out_shape=_mix_out_shape(n) + [jax.ShapeDtypeStruct((SUBLANES, n), F32),
                                       jax.ShapeDtypeStruct((db, N_HEADS, V_DIM), F32)],
        compiler_params=_cparams(1),
        name="mix_prompt",
    )(page_table, x, att, u, u, cwp, cb, lg, lb, wo_b, g2, w1, w2, q8, qt, kn8, vn8, lam_p, sg,
      *([ckt] * (per_step * n_pages)), *([cvf] * (per_step * n_pages)))


def _mix_decode(x, att, u, state_t, layer, cwp, cb, lg, lb, wo_b, g2, w1, w2, tm):
    n = x.shape[0]
    row, tail_in, out_specs = _mix_common_specs(tm, layer)
    return pl.pallas_call(
        _mix_decode_kernel,
        grid=(n // tm,),
        in_specs=[pl.BlockSpec((tm, D_MODEL), row), pl.BlockSpec((tm, ATTN_WIDTH), row),
                  pl.BlockSpec((tm, CONV_CH), row),
                  pl.BlockSpec((None, CONV_BUF, tm, CONV_CH), lambda i: (layer, 0, i, 0))] + tail_in,
        out_specs=out_specs + [pl.BlockSpec((CONV_BUF, tm, CONV_CH), lambda i: (0, i, 0))],
        out_shape=_mix_out_shape(n) + [jax.ShapeDtypeStruct((CONV_BUF, n, CONV_CH), F32)],
        compiler_params=_cparams(1),
        name="mix_decode",
    )(x, att, u, state_t, cwp, cb, lg, lb, wo_b, g2, w1, w2)


def _moe_kernel(x_ref, h_ref, gate_ref, wg_ref, wu_ref, wd_ref, o_ref):
    e = pl.program_id(1)

    @pl.when(e == 0)
    def _():
        o_ref[...] = x_ref[...]

    h = h_ref[...]
    hg = jnp.dot(h, wg_ref[...], preferred_element_type=F32)
    hu = jnp.dot(h, wu_ref[...], preferred_element_type=F32)
    gates = gate_ref[...]
    lane = lax.broadcasted_iota(jnp.int32, gates.shape, 1)
    ge = jnp.sum(jnp.where(lane == e, gates, 0.0), axis=1, keepdims=True)
    act = (hg * jax.nn.sigmoid(hg)) * hu * ge
    o_ref[...] += jnp.dot(act.astype(BF16), wd_ref[...], preferred_element_type=F32)


def _moe(x, h, gates, layer, wg_b, wu_b, wd_b, tm):
    n = x.shape[0]
    row = lambda i, e: (i, 0)
    return pl.pallas_call(
        _moe_kernel,
        grid=(n // tm, N_EXPERTS),
        in_specs=[
            pl.BlockSpec((tm, D_MODEL), row),
            pl.BlockSpec((tm, D_MODEL), row),
            pl.BlockSpec((tm, LANES), row),
            pl.BlockSpec((None, None, D_MODEL, D_EXPERT), lambda i, e: (layer, e, 0, 0)),
            pl.BlockSpec((None, None, D_MODEL, D_EXPERT), lambda i, e: (layer, e, 0, 0)),
            pl.BlockSpec((None, None, D_EXPERT, D_MODEL), lambda i, e: (layer, e, 0, 0)),
        ],
        out_specs=pl.BlockSpec((tm, D_MODEL), row),
        out_shape=jax.ShapeDtypeStruct((n, D_MODEL), F32),
        compiler_params=_cparams(2),
        name="moe",
    )(x, h, gates, wg_b, wu_b, wd_b)


def _moe_sparse_kernel(x_ref, h_ref, gate_ref, grow_ref, wg_ref, wu_ref, wd_ref, o_ref, lt_scr, dcol_scr, drow_scr,
                       cnt_scr, *, rp):
    i = pl.program_id(0)
    g = pl.program_id(1)
    tb = h_ref.shape[0]

    @pl.when((i == 0) & (g == 0))
    def _():
        r = lax.broadcasted_iota(jnp.int32, (tb, tb), 0)
        c = lax.broadcasted_iota(jnp.int32, (tb, tb), 1)
        lt_scr[...] = jnp.where(c < r, 1.0, 0.0).astype(BF16)

    @pl.when(g == 0)
    def _():
        o_ref[...] = x_ref[...]

    gf = g.astype(F32)
    gates = gate_ref[...]
    lane = lax.broadcasted_iota(jnp.int32, gates.shape, 1)
    gcol = jnp.sum(jnp.where(lane == GROUP_LANE, gates, 0.0), axis=1, keepdims=True)
    grow = grow_ref[0:1, :]

    @pl.when(g == 0)
    def _():
        lt = lt_scr[...]
        oh_c = jnp.where(gcol == lane.astype(F32), 1.0, 0.0)
        rk_c = jnp.dot(lt, oh_c.astype(BF16), preferred_element_type=F32)
        dcol_scr[...] = jnp.sum(oh_c * rk_c, axis=1, keepdims=True)
        sub = lax.broadcasted_iota(jnp.int32, (SUBLANES, tb), 0).astype(F32)
        oh_r = jnp.where(grow == sub, 1.0, 0.0)
        rk_r = lax.dot_general(oh_r.astype(BF16), lt, (((1,), (1,)), ((), ())), preferred_element_type=F32)
        drow_scr[...] = jnp.broadcast_to(jnp.sum(oh_r * rk_r, axis=0, keepdims=True), (SUBLANES, tb))
        for l in range(N_GROUPS):
            cnt_scr[l] = jnp.sum(oh_r[l:l + 1, :]).astype(jnp.int32)

    dcol = jnp.where(gcol == gf, dcol_scr[...], -1.0)
    drow = jnp.where(grow == gf, drow_scr[0:1, :], -1.0)
    cnt = cnt_scr[g]
    half = rp // 2
    rem = cnt % rp
    nbig = cnt // rp + (rem > half).astype(jnp.int32)

    g_hi = gates.astype(BF16)
    g_lo = (gates - g_hi.astype(F32)).astype(BF16)
    h = h_ref[...]

    def chunk(first_slot, rows):
        base = first_slot.astype(F32)
        lane_r = lax.broadcasted_iota(jnp.int32, (rows, LANES), 1)
        riota = lax.broadcasted_iota(jnp.int32, (rows, 1), 0).astype(F32)
        ciota = lax.broadcasted_iota(jnp.int32, (1, rows), 1).astype(F32)
        pk = jnp.where(drow == riota + base, 1.0, 0.0).astype(BF16)
        xs = jnp.dot(pk, h, preferred_element_type=F32).astype(BF16)
        gs = (jnp.dot(pk, g_hi, preferred_element_type=F32)
              + jnp.dot(pk, g_lo, preferred_element_type=F32))
        y = jnp.zeros((rows, o_ref.shape[1]), F32)
        for e in range(EXP_PER_GROUP):
            hg = jnp.dot(xs, wg_ref[e], preferred_element_type=F32)
            hu = jnp.dot(xs, wu_ref[e], preferred_element_type=F32)
            ge = jnp.sum(jnp.where(lane_r == g * EXP_PER_GROUP + e, gs, 0.0), axis=1, keepdims=True)
            act = (hg * jax.nn.sigmoid(hg)) * hu * ge
            y = y + jnp.dot(act.astype(BF16), wd_ref[e], preferred_element_type=F32)
        ptk = jnp.where(dcol == ciota + base, 1.0, 0.0).astype(BF16)
        o_ref[...] += jnp.dot(ptk, y.astype(BF16), preferred_element_type=F32)

    def big(k, carry):
        chunk(k * rp, rp)
        return carry

    lax.fori_loop(0, nbig, big, 0)

    @pl.when((rem > 0) & (rem <= half))
    def _():
        chunk(nbig * rp, half)


def _moe_sparse(x, h, gates, grow, layer, wg_b, wu_b, wd_b, tb, rp):
    n, d = x.shape
    de = wg_b.shape[-1]
    row = lambda i, g: (i, 0)
    kern = functools.partial(_moe_sparse_kernel, rp=rp)
    return pl.pallas_call(
        kern,
        grid=(n // tb, N_GROUPS),
        in_specs=[
            pl.BlockSpec((tb, d), row),
            pl.BlockSpec((tb, d), row),
            pl.BlockSpec((tb, LANES), row),
            pl.BlockSpec((SUBLANES, tb), lambda i, g: (0, i)),
            pl.BlockSpec((None, EXP_PER_GROUP, d, de), lambda i, g: (layer, g, 0, 0)),
            pl.BlockSpec((None, EXP_PER_GROUP, d, de), lambda i, g: (layer, g, 0, 0)),
            pl.BlockSpec((None, EXP_PER_GROUP, de, d), lambda i, g: (layer, g, 0, 0)),
        ],
        out_specs=pl.BlockSpec((tb, d), row),
        out_shape=jax.ShapeDtypeStruct((n, d), F32),
        scratch_shapes=[pltpu.VMEM((tb, tb), BF16), pltpu.VMEM((tb, 1), F32), pltpu.VMEM((SUBLANES, tb), F32),
                        pltpu.SMEM((N_GROUPS,), jnp.int32)],
        compiler_params=_cparams(2),
        name="moe_sparse",
    )(x, h, gates, grow, wg_b, wu_b, wd_b)


def _rope_tables(pos):
    half = ROT_DIM // 2
    inv = jnp.power(ROPE_THETA, -jnp.arange(0, ROT_DIM, 2, dtype=F32) / ROT_DIM)
    ang = pos.astype(F32)[:, None] * inv[None, :]
    cos, sin = jnp.cos(ang), jnp.sin(ang)
    n = pos.shape[0]
    pad = jnp.zeros((n, HEAD_DIM - ROT_DIM), F32)
    zer = jnp.zeros((n, half), F32)
    c = jnp.concatenate([cos, cos, pad + 1.0], axis=1)
    s1 = jnp.concatenate([-sin, zer, pad], axis=1)
    s2 = jnp.concatenate([zer, sin, pad], axis=1)
    rep = LANES // HEAD_DIM
    return jnp.tile(c, (1, rep)), jnp.tile(s1, (1, rep)), jnp.tile(s2, (1, rep))


def kernel(x_prompt, x_sample, cache_k, cache_v, state_conv, page_table, norm1_g, w_in, q_norm_g, k_norm_g,
           lam_q1, lam_k1, lam_q2, lam_k2, subln_g, conv_w, conv_b, conv_ln_g, conv_ln_b, w_out, norm2_g,
           w_router_group, w_router_expert, w_gate, w_up, w_down):
    batch, seq, d = x_prompt.shape
    db = x_sample.shape[0]
    depth = w_in.shape[0]
    n_pages, page = page_table.shape[1], cache_k.shape[2]
    past = n_pages * page
    n = batch * seq

    xp = x_prompt.reshape(n, d)
    xs = x_sample.reshape(db, d)
    tabs_p = _rope_tables(jnp.arange(seq, dtype=jnp.int32))
    tabs_s = _rope_tables(jnp.full((db,), past, dtype=jnp.int32))
    gi = jnp.arange(ATTN_WIDTH, dtype=jnp.int32) // HEAD_DIM
    bd = (gi[:, None] == gi[None, :]).astype(BF16)
    ckt = jnp.transpose(cache_k, (0, 1, 3, 4, 2))
    cvf = cache_v.reshape(depth, cache_v.shape[1], page * N_HEADS, V_DIM)
    state_t = jnp.transpose(state_conv, (0, 2, 1, 3))
    w_in_b, wo_b = w_in.astype(BF16), w_out.astype(BF16)
    wg_b, wu_b, wd_b = w_gate.astype(BF16), w_up.astype(BF16), w_down.astype(BF16)
    kvp = kvs = None

    cp_l, cs_l = [], []
    for l in range(depth):
        lam_init = 0.8 - 0.6 * math.exp(-0.3 * l)
        g1 = norm1_g[l][None, :]
        qg = jnp.tile(q_norm_g[l], N_SUB)[None, :]
        kg = jnp.tile(k_norm_g[l], N_SUB)[None, :]
        lam_p = jnp.stack([lam_q1[l], lam_k1[l], lam_q2[l], lam_k2[l]])
        sg = subln_g[l][None, :]
        cwp = jnp.pad(conv_w[l], ((0, HALO - CONV_W), (0, 0)))
        cb, lg, lb = conv_b[l][None, :], conv_ln_g[l][None, :], conv_ln_b[l][None, :]
        g2 = norm2_g[l][None, :]
        wr = jnp.concatenate([w_router_group[l], w_router_expert[l]], axis=1)
        wr_hi = wr.astype(BF16)
        wr_lo = (wr - wr_hi.astype(F32)).astype(BF16)
        nr = wr.shape[1]
        w1 = jnp.zeros((d, LANES), BF16).at[:, :nr].set(wr_hi).at[:, LANES // 2:LANES // 2 + nr].set(wr_lo)
        w2 = jnp.zeros((d, LANES), BF16).at[:, :nr].set(wr_hi)

        q, k_all, v_all, u, kb = _inproj(xp, l, depth, kvp, g1, w_in_b, bd, qg, kg, *tabs_p, tm=512)
        kvp = (k_all, v_all)
        qs, ks_all, vs_all, us, _ = _inproj(xs, l, depth, kvs, g1, w_in_b, bd, qg, kg, *tabs_s, tm=db)
        kvs = (ks_all, vs_all)
        q8 = qs.astype(F32).reshape(db, N_SUB, HEAD_DIM)
        vn8 = jnp.repeat(vs_all[l].reshape(db, N_HEADS, V_DIM), 2, axis=1)
        kn8 = jnp.transpose(ks_all[l, 0].reshape(N_SUB, HEAD_DIM, db), (2, 0, 1))

        att = _prompt_attention(q, kb, v_all, l, lam_p, sg.reshape(V_DIM, 1), batch, seq, 512, lam_init)
        xp, h2, gates, grow, att_s = _mix_prompt(xp, att, u, l, cwp, cb, lg, lb, wo_b, g2, w1, w2, seq, 256,
                                                 page_table, q8, jnp.transpose(q8, (0, 2, 1)), kn8, vn8, lam_p, sg,
                                                 ckt, cvf, lam_init)
        xp = _moe_sparse(xp, h2, gates, grow, l, wg_b, wu_b, wd_b, 1024, 256)
        cp_l.append(u.reshape(batch, seq, CONV_CH)[:, seq - CONV_BUF:])

        att_s = att_s.reshape(db, ATTN_WIDTH).astype(BF16)
        xs, h2, gates, ns = _mix_decode(xs, att_s, us, state_t, l, cwp, cb, lg, lb, wo_b, g2, w1, w2, 64)
        xs = _moe(xs, h2, gates, l, wg_b, wu_b, wd_b, db)
        cs_l.append(jnp.transpose(ns, (1, 0, 2)))

    k_prompt = jnp.transpose(kvp[0].reshape(depth, batch, N_SUB, HEAD_DIM, seq), (0, 1, 4, 2, 3))
    k_sample = jnp.transpose(kvs[0].reshape(depth, 1, N_SUB, HEAD_DIM, db), (0, 4, 1, 2, 3))
    return (xp.reshape(batch, seq, d), xs.reshape(db, 1, d),
            k_prompt, kvp[1].reshape(depth, batch, seq, N_HEADS, V_DIM), jnp.stack(cp_l),
            k_sample, kvs[1].reshape(depth, db, 1, N_HEADS, V_DIM), jnp.stack(cs_l))
```

```python
import functools
import math

import jax
import jax.numpy as jnp
from jax import lax
from jax.experimental import pallas as pl
from jax.experimental.pallas import tpu as pltpu

F32 = jnp.float32
BF16 = jnp.bfloat16

D_MODEL = 1024
HEAD_DIM = 64
N_SUB = 8
N_HEADS = 4
V_DIM = 128
ATTN_WIDTH = 512
CONV_CH = 512
ROT_DIM = 16
ROPE_THETA = 500000.0
CONV_W = 31
CONV_BUF = CONV_W - 1
N_GROUPS = 4
EXP_PER_GROUP = 4
N_EXPERTS = 16
D_EXPERT = 512
EPS = 1e-6
D_IN = 3 * ATTN_WIDTH + 2 * CONV_CH

LANES = 128
SUBLANES = 8
HALO = 32
EXT_TAIL = 16
Q_SCALE = HEAD_DIM ** -0.5 * math.log2(math.e)
QK_AHEAD = 4
GROUP_LANE = N_EXPERTS
VMEM_LIMIT = 56 * 1024 * 1024


def _cparams(n_axes):
    return pltpu.CompilerParams(dimension_semantics=("arbitrary",) * n_axes,
                                vmem_limit_bytes=VMEM_LIMIT)


def _inproj_kernel(x_ref, g1_ref, w_ref, bd_ref, qg_ref, kg_ref, c_ref, s1_ref, s2_ref, *rest):
    q_ref, k_ref, v_ref, u_ref, kb_ref = rest[-5:]
    x = x_ref[...]
    ms = jnp.mean(x * x, axis=-1, keepdims=True)
    h = (x * lax.rsqrt(ms + EPS) * g1_ref[...]).astype(BF16)
    c, s1, s2 = c_ref[...], s1_ref[...], s2_ref[...]
    bd = bd_ref[...]

    def proj(lo, width):
        return jnp.dot(h, w_ref[:, lo:lo + width], preferred_element_type=F32)

    def head_ms(t):
        return jnp.dot((t * t).astype(BF16), bd, preferred_element_type=F32) * (1.0 / HEAD_DIM)

    def norm_rope(t, hms, g):
        tn = t * lax.rsqrt(hms + EPS) * g
        outs = []
        for j in range(ATTN_WIDTH // LANES):
            blk = tn[:, j * LANES:(j + 1) * LANES]
            up = pltpu.roll(blk, LANES - ROT_DIM // 2, 1)
            dn = pltpu.roll(blk, ROT_DIM // 2, 1)
            outs.append(blk * c + up * s1 + dn * s2)
        return outs

    zq = proj(0, ATTN_WIDTH)
    zk = proj(ATTN_WIDTH, ATTN_WIDTH)
    ms_q = head_ms(zq)
    zv = proj(2 * ATTN_WIDTH, ATTN_WIDTH)
    ms_k = head_ms(zk)
    za = proj(3 * ATTN_WIDTH, CONV_CH)
    zg = proj(3 * ATTN_WIDTH + CONV_CH, CONV_CH)
    qs = norm_rope(zq, ms_q, qg_ref[...])
    ks = norm_rope(zk, ms_k, kg_ref[...])
    for j in range(ATTN_WIDTH // LANES):
        sl = slice(j * LANES, (j + 1) * LANES)
        q_ref[:, sl] = (qs[j] * Q_SCALE).astype(BF16)
        k_ref[sl, :] = ks[j].T
        kb_ref[:, sl] = ks[j].astype(BF16)
    tm = x_ref.shape[0]
    for hd in range(N_HEADS):
        v_ref[pl.ds(hd, tm, stride=N_HEADS), :] = zv[:, hd * V_DIM:(hd + 1) * V_DIM]
    u_ref[...] = za * jax.nn.sigmoid(zg)


def _inproj(x, layer, depth, kv_prev, g1, w_in_b, bd, qg, kg, ct, s1t, s2t, tm):
    n = x.shape[0]
    npos = ct.shape[0] // tm
    row = lambda i: (i, 0)
    full = lambda i: (0, 0)
    pos = lambda i: (i % npos, 0)
    lrow = lambda i: (layer, i, 0)
    in_specs = [
        pl.BlockSpec((tm, D_MODEL), row),
        pl.BlockSpec((1, D_MODEL), full),
        pl.BlockSpec((None, D_MODEL, D_IN), lambda i: (layer, 0, 0)),
        pl.BlockSpec((ATTN_WIDTH, ATTN_WIDTH), full),
        pl.BlockSpec((1, ATTN_WIDTH), full),
        pl.BlockSpec((1, ATTN_WIDTH), full),
        pl.BlockSpec((tm, LANES), pos),
        pl.BlockSpec((tm, LANES), pos),
        pl.BlockSpec((tm, LANES), pos),
    ]
    args = [x, g1, w_in_b, bd, qg, kg, ct, s1t, s2t]
    aliases = {len(args): 1, len(args) + 1: 2}
    in_specs += [pl.BlockSpec(memory_space=pl.ANY)] * 2
    args += list(kv_prev)
    return pl.pallas_call(
        _inproj_kernel,
        grid=(n // tm,),
        in_specs=in_specs,
        out_specs=[pl.BlockSpec((tm, ATTN_WIDTH), row),
                   pl.BlockSpec((None, None, ATTN_WIDTH, tm), lambda i: (layer, i // npos, 0, i % npos)),
                   pl.BlockSpec((None, tm * N_HEADS, V_DIM), lrow),
                   pl.BlockSpec((tm, CONV_CH), row),
                   pl.BlockSpec((tm, ATTN_WIDTH), row)],
        out_shape=[
            jax.ShapeDtypeStruct((n, ATTN_WIDTH), BF16),
            jax.ShapeDtypeStruct((depth, n // (npos * tm), ATTN_WIDTH, npos * tm), F32),
            jax.ShapeDtypeStruct((depth, n * N_HEADS, V_DIM), F32),
            jax.ShapeDtypeStruct((n, CONV_CH), F32),
            jax.ShapeDtypeStruct((n, ATTN_WIDTH), BF16),
        ],
        input_output_aliases=aliases,
        compiler_params=_cparams(1),
        name="inproj",
    )(*args)


def _lambda_value(lam_ref, lam_init):
    lp = lam_ref[...]
    t1 = jnp.sum(lp[0:1] * lp[1:2], axis=1, keepdims=True)
    t2 = jnp.sum(lp[2:3] * lp[3:4], axis=1, keepdims=True)
    return jnp.exp(t1) - jnp.exp(t2) + lam_init


def _subln(o, g, lam_init):
    ms = jnp.mean(o * o, axis=-1, keepdims=True)
    return o * lax.rsqrt(ms + EPS) * g * (1.0 - lam_init)


def _pattn_kernel(q_ref, k_ref, v_ref, lam_ref, sgc_ref, o_ref, vt_scr, qs_scr, m_scr, l_scr, acc_scr,
                  *, tq, lam_init):
    i = pl.program_id(1)
    nkv = k_ref.shape[0] // tq

    @pl.when(i == 0)
    def _():
        for c in range(nkv):
            for hd in range(N_HEADS):
                rows = v_ref[pl.ds(c * tq * N_HEADS + hd, tq, stride=N_HEADS), :]
                vt_scr[c, hd * V_DIM:(hd + 1) * V_DIM, :] = rows.T.astype(BF16)

    lam = _lambda_value(lam_ref, lam_init)
    lane = lax.broadcasted_iota(jnp.int32, (tq, LANES), 1)
    causal = (lax.broadcasted_iota(jnp.int32, (tq, tq), 0) <= lax.broadcasted_iota(jnp.int32, (tq, tq), 1))

    for sh in range(N_SUB):
        qp = q_ref[:, (sh // 2) * LANES:(sh // 2 + 1) * LANES]
        keep = (lane < HEAD_DIM) if sh % 2 == 0 else (lane >= HEAD_DIM)
        qs_scr[sh] = jnp.where(keep, qp, jnp.zeros_like(qp))
        m_scr[sh] = jnp.full((1, tq), -jnp.inf, F32)
        l_scr[sh] = jnp.zeros((1, tq), F32)
        acc_scr[sh] = jnp.zeros((V_DIM, tq), F32)

    def step(j, masked):
        off = pl.multiple_of(j * tq, tq)
        hq = tq // 2
        chains = [(sh, qh) for sh in range(N_SUB) for qh in range(2)]

        def nkeys(qh):
            return hq if (masked and qh == 0) else tq

        def scores(sh, qh):
            kj = k_ref[pl.ds(off, nkeys(qh)), (sh // 2) * LANES:(sh // 2 + 1) * LANES]
            return lax.dot_general(kj, qs_scr[sh, qh * hq:(qh + 1) * hq, :], (((1,), (1,)), ((), ())),
                                   preferred_element_type=F32)

        pending = [scores(*chains[c]) for c in range(QK_AHEAD)]
        for c, (sh, qh) in enumerate(chains):
            sl = slice((sh // 2) * LANES, (sh // 2 + 1) * LANES)
            cols = slice(qh * hq, (qh + 1) * hq)
            nk = nkeys(qh)
            st = pending.pop(0)
            if c + QK_AHEAD < len(chains):
                pending.append(scores(*chains[c + QK_AHEAD]))
            if masked:
                st = jnp.where(causal[0:nk, cols], st, -jnp.inf)
            m_prev = m_scr[sh, :, cols]
            m_new = jnp.maximum(m_prev, jnp.max(st, axis=0, keepdims=True))
            alpha = jnp.exp2(m_prev - m_new)
            p = jnp.exp2(st - m_new)
            l_scr[sh, :, cols] = alpha * l_scr[sh, :, cols] + jnp.sum(p, axis=0, keepdims=True)
            acc_scr[sh, :, cols] = alpha * acc_scr[sh, :, cols] + jnp.dot(vt_scr[j, sl, 0:nk], p.astype(BF16),
                                                                          preferred_element_type=F32)
            m_scr[sh, :, cols] = m_new

    def body(j, carry):
        step(j, False)
        return carry

    lax.fori_loop(0, i, body, 0)
    step(i, True)

    sgc = sgc_ref[...]
    for hd in range(N_HEADS):
        od = acc_scr[2 * hd] / l_scr[2 * hd] - lam * (acc_scr[2 * hd + 1] / l_scr[2 * hd + 1])
        ms = jnp.mean(od * od, axis=0, keepdims=True)
        on = od * lax.rsqrt(ms + EPS) * sgc * (1.0 - lam_init)
        o_ref[:, hd * LANES:(hd + 1) * LANES] = on.T.astype(BF16)


def _prompt_attention(q, kb, v_all, layer, lam_p, sgc, batch, seq, tq, lam_init):
    nq = seq // tq
    n = batch * seq
    kern = functools.partial(_pattn_kernel, tq=tq, lam_init=lam_init)
    return pl.pallas_call(
        kern,
        grid=(batch, nq),
        in_specs=[
            pl.BlockSpec((tq, ATTN_WIDTH), lambda b, i: (b * nq + i, 0)),
            pl.BlockSpec((seq, ATTN_WIDTH), lambda b, i: (b, 0)),
            pl.BlockSpec((None, seq * N_HEADS, V_DIM), lambda b, i: (layer, b, 0)),
            pl.BlockSpec((4, HEAD_DIM), lambda b, i: (0, 0)),
            pl.BlockSpec((V_DIM, 1), lambda b, i: (0, 0)),
        ],
        out_specs=pl.BlockSpec((tq, ATTN_WIDTH), lambda b, i: (b * nq + i, 0)),
        out_shape=jax.ShapeDtypeStruct((n, ATTN_WIDTH), BF16),
        scratch_shapes=[
            pltpu.VMEM((nq, ATTN_WIDTH, tq), BF16),
            pltpu.VMEM((N_SUB, tq, LANES), BF16),
            pltpu.VMEM((N_SUB, 1, tq), F32),
            pltpu.VMEM((N_SUB, 1, tq), F32),
            pltpu.VMEM((N_SUB, V_DIM, tq), F32),
        ],
        compiler_params=_cparams(2),
        name="prompt_attn",
    )(q, kb, v_all, lam_p, sgc)


def _decode_scores(q8, qt, kn, k_refs, page):
    qcols = [jnp.broadcast_to(qt[:, h:h + 1], (HEAD_DIM, page)) for h in range(N_SUB)]
    s = jnp.concatenate(
        [jnp.concatenate([jnp.sum(k_ref[h] * qcols[h], axis=0, keepdims=True) for h in range(N_SUB)], axis=0)
         for k_ref in k_refs], axis=1)
    s_self = jnp.sum(q8 * kn, axis=1, keepdims=True)
    return s, s_self


def _decode_probs(s, s_self, lam):
    m = jnp.maximum(jnp.max(s, axis=1, keepdims=True), s_self)
    pexp = jnp.exp2(s - m)
    pself = jnp.exp2(s_self - m)
    inv = 1.0 / (jnp.sum(pexp, axis=1, keepdims=True) + pself)
    sub = lax.broadcasted_iota(jnp.int32, (N_SUB, 1), 0)
    coef = jnp.where((sub & 1) == 0, 1.0, -lam) * inv
    a8 = pexp * coef
    a8 = (a8 + pltpu.roll(a8, N_SUB - 1, 0)).astype(BF16)
    as8 = jnp.broadcast_to(pself * coef, (N_SUB, V_DIM))
    return a8, as8 + pltpu.roll(as8, N_SUB - 1, 0)


def _decode_values(a8, as8, vn, sg, v_refs, page, lam_init):
    o_self = as8 * vn
    rows = []
    for hd in range(N_HEADS):
        v_hd = jnp.concatenate([v_ref[pl.ds(hd, page, stride=N_HEADS), :].astype(BF16) for v_ref in v_refs], axis=0)
        acc = jnp.dot(a8, v_hd, preferred_element_type=F32)
        rows.append(acc[2 * hd:2 * hd + 1] + o_self[2 * hd:2 * hd + 1])
    o = jnp.concatenate(rows, axis=0)
    return _subln(o, sg, lam_init)


def _conv_post(y, lg_ref, lb_ref):
    mu = jnp.mean(y, axis=-1, keepdims=True)
    yc = y - mu
    var = jnp.mean(yc * yc, axis=-1, keepdims=True)
    yn = yc * lax.rsqrt(var + EPS) * lg_ref[...] + lb_ref[...]
    return yn * jax.nn.sigmoid(yn)


def _mix_tail(x, att, cnv, wo_ref, g2_ref, w1_ref, w2_ref, xo_ref, h_ref, gate_ref, grow_ref=None):
    y = (x + jnp.dot(att, wo_ref[0:ATTN_WIDTH, :], preferred_element_type=F32)
         + jnp.dot(cnv.astype(BF16), wo_ref[ATTN_WIDTH:, :], preferred_element_type=F32))
    xo_ref[...] = y
    ms = jnp.mean(y * y, axis=-1, keepdims=True)
    h = y * lax.rsqrt(ms + EPS) * g2_ref[...]
    h_hi = h.astype(BF16)
    h_lo = (h - h_hi.astype(F32)).astype(BF16)
    h_ref[...] = h_hi
    r = jnp.dot(h_hi, w1_ref[...], preferred_element_type=F32)
    lg = r + pltpu.roll(r, LANES // 2, 1) + jnp.dot(h_lo, w2_ref[...], preferred_element_type=F32)
    tm = lg.shape[0]
    lane = lax.broadcasted_iota(jnp.int32, (tm, LANES), 1)
    lanef = lane.astype(F32)
    big = float(LANES)
    neg = -jnp.inf
    gmask = lane < N_GROUPS
    gl = jnp.where(gmask, lg, neg)
    gmax = jnp.max(gl, axis=1, keepdims=True)
    gsum = jnp.sum(jnp.where(gmask, jnp.exp(gl - gmax), 0.0), axis=1, keepdims=True)
    g_w = 1.0 / gsum
    g_idx = jnp.min(jnp.where(gl == gmax, lanef, big), axis=1, keepdims=True)
    lo = N_GROUPS + EXP_PER_GROUP * g_idx
    emask = (lanef >= lo) & (lanef < lo + EXP_PER_GROUP)
    ev = jnp.where(emask, lg, neg)
    v1 = jnp.max(ev, axis=1, keepdims=True)
    i1 = jnp.min(jnp.where(ev == v1, lanef, big), axis=1, keepdims=True)
    ev2 = jnp.where(lanef == i1, neg, ev)
    v2 = jnp.max(ev2, axis=1, keepdims=True)
    i2 = jnp.min(jnp.where(ev2 == v2, lanef, big), axis=1, keepdims=True)
    e2 = jnp.exp(v2 - v1)
    den = 1.0 / (1.0 + e2)
    w1 = den * g_w
    w2 = e2 * den * g_w
    gates = jnp.where(lanef == i1, w1, jnp.where(lanef == i2, w2, 0.0))
    gates = pltpu.roll(gates, LANES - N_GROUPS, 1)
    gate_ref[...] = jnp.where(lane == GROUP_LANE, g_idx, gates)
    if grow_ref is not None:
        grow_ref[...] = jnp.transpose(jnp.broadcast_to(g_idx, (tm, LANES)))[0:SUBLANES, :]


def _mix_prompt_kernel(pt_ref, x_ref, att_ref, u_ref, uh_ref, cw_ref, cb_ref, lg_ref, lb_ref, wo_ref, g2_ref,
                       w1_ref, w2_ref, q8_ref, qt_ref, kn_ref, vn_ref, lam_ref, sg_ref, ck_hbm, cv_hbm,
                       xo_ref, h_ref, gate_ref, grow_ref, od_ref, ext_scr, kbuf, vbuf, sem,
                       *, tm, tiles_per_seq, layer, steps, n_pages, page, per_step, lam_init):
    nkv = per_step * n_pages

    def page_copy(slot, j, phys):
        return (pltpu.make_async_copy(ck_hbm.at[layer, phys], kbuf.at[slot, j], sem.at[slot]),
                pltpu.make_async_copy(cv_hbm.at[layer, phys], vbuf.at[slot, j], sem.at[slot]))

    def start_pages(step, slot):
        for j in range(nkv):
            for cp in page_copy(slot, j, pt_ref[step * per_step + j // n_pages, j % n_pages]):
                cp.start()

    i = pl.program_id(0)
    slot = i % 2

    @pl.when(i == 0)
    def _():
        start_pages(0, 0)

    for j in range(nkv):
        for cp in page_copy(slot, j, 0):
            cp.wait()

    @pl.when(i + 1 < steps)
    def _():
        start_pages(i + 1, 1 - slot)

    k_refs = [kbuf.at[slot, j] for j in range(nkv)]
    v_refs = [vbuf.at[slot, j] for j in range(nkv)]
    first = (i % tiles_per_seq) == 0
    ext_scr[0:HALO, :] = jnp.where(first, 0.0, uh_ref[...])
    ext_scr[HALO:HALO + tm, :] = u_ref[...]
    ext_scr[HALO + tm:, :] = jnp.zeros((EXT_TAIL, CONV_CH), F32)
    lead = HALO - CONV_BUF
    acc = jnp.zeros((tm, CONV_CH), F32) + cb_ref[...]
    for r in range(SUBLANES):
        z = None
        for a in range((CONV_W + lead + SUBLANES - 1) // SUBLANES):
            j = SUBLANES * a + r - lead
            if 0 <= j < CONV_W:
                term = cw_ref[j:j + 1, :] * ext_scr[SUBLANES * a:SUBLANES * a + tm + SUBLANES, :]
                z = term if z is None else z + term
        acc = acc + z[r:r + tm]
    cnv = _conv_post(acc, lg_ref, lb_ref)
    _mix_tail(x_ref[...], att_ref[...], cnv, wo_ref, g2_ref, w1_ref, w2_ref, xo_ref, h_ref, gate_ref, grow_ref)
    lam = _lambda_value(lam_ref, lam_init)
    pages = [slice(sq * n_pages, (sq + 1) * n_pages) for sq in range(per_step)]
    scores = [_decode_scores(q8_ref[sq], qt_ref[sq], kn_ref[sq], k_refs[pages[sq]], page) for sq in range(per_step)]
    probs = [_decode_probs(*scores[sq], lam) for sq in range(per_step)]
    for sq in range(per_step):
        od_ref[sq] = _decode_values(*probs[sq], vn_ref[sq], sg_ref[...], v_refs[pages[sq]], page, lam_init)


def _mix_decode_kernel(x_ref, att_ref, u_ref, st_ref, cw_ref, cb_ref, lg_ref, lb_ref, wo_ref, g2_ref,
                       w1_ref, w2_ref, xo_ref, h_ref, gate_ref, ns_ref):
    u = u_ref[...]
    acc = cb_ref[...] + cw_ref[CONV_BUF:CONV_BUF + 1, :] * u
    for j in range(CONV_BUF):
        acc = acc + cw_ref[j:j + 1, :] * st_ref[j]
    for j in range(CONV_BUF - 1):
        ns_ref[j] = st_ref[j + 1]
    ns_ref[CONV_BUF - 1] = u
    cnv = _conv_post(acc, lg_ref, lb_ref)
    _mix_tail(x_ref[...], att_ref[...], cnv, wo_ref, g2_ref, w1_ref, w2_ref, xo_ref, h_ref, gate_ref)


def _mix_common_specs(tm, layer):
    row = lambda i, *_: (i, 0)
    full = lambda i, *_: (0, 0)
    tail_in = [
        pl.BlockSpec((HALO, CONV_CH), full),
        pl.BlockSpec((1, CONV_CH), full),
        pl.BlockSpec((1, CONV_CH), full),
        pl.BlockSpec((1, CONV_CH), full),
        pl.BlockSpec((None, D_MODEL, D_MODEL), lambda i, *_: (layer, 0, 0)),
        pl.BlockSpec((1, D_MODEL), full),
        pl.BlockSpec((D_MODEL, LANES), full),
        pl.BlockSpec((D_MODEL, LANES), full),
    ]
    out_specs = [pl.BlockSpec((tm, D_MODEL), row), pl.BlockSpec((tm, D_MODEL), row),
                 pl.BlockSpec((tm, LANES), row)]
    return row, tail_in, out_specs


def _mix_out_shape(n):
    return [jax.ShapeDtypeStruct((n, D_MODEL), F32), jax.ShapeDtypeStruct((n, D_MODEL), BF16),
            jax.ShapeDtypeStruct((n, LANES), F32)]


def _mix_prompt(x, att, u, layer, cwp, cb, lg, lb, wo_b, g2, w1, w2, seq, tm,
                page_table, q8, qt, kn8, vn8, lam_p, sg, ckt, cvf, lam_init):
    n = x.shape[0]
    steps = n // tm
    db, n_pages = page_table.shape
    page = ckt.shape[-1]
    per_step = db // steps
    assert per_step * steps == db
    row, tail_in, out_specs = _mix_common_specs(tm, layer)
    halo_map = lambda i, pt: (jnp.maximum(i * (tm // HALO) - 1, 0), 0)
    seq3 = lambda i, pt: (i, 0, 0)
    const2 = lambda i, pt: (0, 0)

    in_specs = [pl.BlockSpec((tm, D_MODEL), row), pl.BlockSpec((tm, ATTN_WIDTH), row),
                pl.BlockSpec((tm, CONV_CH), row), pl.BlockSpec((HALO, CONV_CH), halo_map)] + tail_in
    in_specs += [
        pl.BlockSpec((per_step, N_SUB, HEAD_DIM), seq3),
        pl.BlockSpec((per_step, HEAD_DIM, N_SUB), seq3),
        pl.BlockSpec((per_step, N_SUB, HEAD_DIM), seq3),
        pl.BlockSpec((per_step, N_SUB, V_DIM), seq3),
        pl.BlockSpec((4, HEAD_DIM), const2),
        pl.BlockSpec((1, V_DIM), const2),
    ]
    in_specs += [pl.BlockSpec(memory_space=pl.ANY)] * 2
    kern = functools.partial(_mix_prompt_kernel, tm=tm, tiles_per_seq=seq // tm, layer=layer, steps=steps,
                             n_pages=n_pages, page=page, per_step=per_step, lam_init=lam_init)
    grid_spec = pltpu.PrefetchScalarGridSpec(
        num_scalar_prefetch=1,
        grid=(steps,),
        in_specs=in_specs,
        out_specs=out_specs + [pl.BlockSpec((SUBLANES, tm), lambda i, pt: (0, i)),
                               pl.BlockSpec((per_step, N_HEADS, V_DIM), seq3)],
        scratch_shapes=[pltpu.VMEM((tm + HALO + EXT_TAIL, CONV_CH), F32),
                        pltpu.VMEM((2, per_step * n_pages, N_SUB, HEAD_DIM, page), F32),
                        pltpu.VMEM((2, per_step * n_pages, page * N_HEADS, V_DIM), F32),
                        pltpu.SemaphoreType.DMA((2,))],
    )
    return pl.pallas_call(
        kern,
        grid_spec=grid_spec,
        out_shape=_mix_out_shape(n) + [jax.ShapeDtypeStruct((SUBLANES, n), F32),
                                       jax.ShapeDtypeStruct((db, N_HEADS, V_DIM), F32)],
        compiler_params=_cparams(1),
        name="mix_prompt",
    )(page_table, x, att, u, u, cwp, cb, lg, lb, wo_b, g2, w1, w2, q8, qt, kn8, vn8, lam_p, sg, ckt, cvf)


def _mix_decode(x, att, u, state_t, layer, cwp, cb, lg, lb, wo_b, g2, w1, w2, tm):
    n = x.shape[0]
    row, tail_in, out_specs = _mix_common_specs(tm, layer)
    return pl.pallas_call(
        _mix_decode_kernel,
        grid=(n // tm,),
        in_specs=[pl.BlockSpec((tm, D_MODEL), row), pl.BlockSpec((tm, ATTN_WIDTH), row),
                  pl.BlockSpec((tm, CONV_CH), row),
                  pl.BlockSpec((None, CONV_BUF, tm, CONV_CH), lambda i: (layer, 0, i, 0))] + tail_in,
        out_specs=out_specs + [pl.BlockSpec((CONV_BUF, tm, CONV_CH), lambda i: (0, i, 0))],
        out_shape=_mix_out_shape(n) + [jax.ShapeDtypeStruct((CONV_BUF, n, CONV_CH), F32)],
        compiler_params=_cparams(1),
        name="mix_decode",
    )(x, att, u, state_t, cwp, cb, lg, lb, wo_b, g2, w1, w2)


def _moe_kernel(x_ref, h_ref, gate_ref, wg_ref, wu_ref, wd_ref, o_ref):
    e = pl.program_id(1)

    @pl.when(e == 0)
    def _():
        o_ref[...] = x_ref[...]

    h = h_ref[...]
    hg = jnp.dot(h, wg_ref[...], preferred_element_type=F32)
    hu = jnp.dot(h, wu_ref[...], preferred_element_type=F32)
    gates = gate_ref[...]
    lane = lax.broadcasted_iota(jnp.int32, gates.shape, 1)
    ge = jnp.sum(jnp.where(lane == e, gates, 0.0), axis=1, keepdims=True)
    act = (hg * jax.nn.sigmoid(hg)) * hu * ge
    o_ref[...] += jnp.dot(act.astype(BF16), wd_ref[...], preferred_element_type=F32)


def _moe(x, h, gates, layer, wg_b, wu_b, wd_b, tm):
    n = x.shape[0]
    row = lambda i, e: (i, 0)
    return pl.pallas_call(
        _moe_kernel,
        grid=(n // tm, N_EXPERTS),
        in_specs=[
            pl.BlockSpec((tm, D_MODEL), row),
            pl.BlockSpec((tm, D_MODEL), row),
            pl.BlockSpec((tm, LANES), row),
            pl.BlockSpec((None, None, D_MODEL, D_EXPERT), lambda i, e: (layer, e, 0, 0)),
            pl.BlockSpec((None, None, D_MODEL, D_EXPERT), lambda i, e: (layer, e, 0, 0)),
            pl.BlockSpec((None, None, D_EXPERT, D_MODEL), lambda i, e: (layer, e, 0, 0)),
        ],
        out_specs=pl.BlockSpec((tm, D_MODEL), row),
        out_shape=jax.ShapeDtypeStruct((n, D_MODEL), F32),
        compiler_params=_cparams(2),
        name="moe",
    )(x, h, gates, wg_b, wu_b, wd_b)


def _moe_sparse_kernel(x_ref, h_ref, gate_ref, grow_ref, wg_ref, wu_ref, wd_ref, o_ref, lt_scr, dcol_scr, drow_scr,
                       cnt_scr, *, rp):
    i = pl.program_id(0)
    g = pl.program_id(1)
    tb = h_ref.shape[0]

    @pl.when((i == 0) & (g == 0))
    def _():
        r = lax.broadcasted_iota(jnp.int32, (tb, tb), 0)
        c = lax.broadcasted_iota(jnp.int32, (tb, tb), 1)
        lt_scr[...] = jnp.where(c < r, 1.0, 0.0).astype(BF16)

    @pl.when(g == 0)
    def _():
        o_ref[...] = x_ref[...]

    gf = g.astype(F32)
    gates = gate_ref[...]
    lane = lax.broadcasted_iota(jnp.int32, gates.shape, 1)
    gcol = jnp.sum(jnp.where(lane == GROUP_LANE, gates, 0.0), axis=1, keepdims=True)
    grow = grow_ref[0:1, :]

    @pl.when(g == 0)
    def _():
        lt = lt_scr[...]
        oh_c = jnp.where(gcol == lane.astype(F32), 1.0, 0.0)
        rk_c = jnp.dot(lt, oh_c.astype(BF16), preferred_element_type=F32)
        dcol_scr[...] = jnp.sum(oh_c * rk_c, axis=1, keepdims=True)
        sub = lax.broadcasted_iota(jnp.int32, (SUBLANES, tb), 0).astype(F32)
        oh_r = jnp.where(grow == sub, 1.0, 0.0)
        rk_r = lax.dot_general(oh_r.astype(BF16), lt, (((1,), (1,)), ((), ())), preferred_element_type=F32)
        drow_scr[...] = jnp.broadcast_to(jnp.sum(oh_r * rk_r, axis=0, keepdims=True), (SUBLANES, tb))
        for l in range(N_GROUPS):
            cnt_scr[l] = jnp.sum(oh_r[l:l + 1, :]).astype(jnp.int32)

    dcol = jnp.where(gcol == gf, dcol_scr[...], -1.0)
    drow = jnp.where(grow == gf, drow_scr[0:1, :], -1.0)
    cnt = cnt_scr[g]
    half = rp // 2
    rem = cnt % rp
    nbig = cnt // rp + (rem > half).astype(jnp.int32)

    g_hi = gates.astype(BF16)
    g_lo = (gates - g_hi.astype(F32)).astype(BF16)
    h = h_ref[...]

    def chunk(first_slot, rows):
        base = first_slot.astype(F32)
        lane_r = lax.broadcasted_iota(jnp.int32, (rows, LANES), 1)
        riota = lax.broadcasted_iota(jnp.int32, (rows, 1), 0).astype(F32)
        ciota = lax.broadcasted_iota(jnp.int32, (1, rows), 1).astype(F32)
        pk = jnp.where(drow == riota + base, 1.0, 0.0).astype(BF16)
        xs = jnp.dot(pk, h, preferred_element_type=F32).astype(BF16)
        gs = (jnp.dot(pk, g_hi, preferred_element_type=F32)
              + jnp.dot(pk, g_lo, preferred_element_type=F32))
        y = jnp.zeros((rows, o_ref.shape[1]), F32)
        for e in range(EXP_PER_GROUP):
            hg = jnp.dot(xs, wg_ref[e], preferred_element_type=F32)
            hu = jnp.dot(xs, wu_ref[e], preferred_element_type=F32)
            ge = jnp.sum(jnp.where(lane_r == g * EXP_PER_GROUP + e, gs, 0.0), axis=1, keepdims=True)
            act = (hg * jax.nn.sigmoid(hg)) * hu * ge
            y = y + jnp.dot(act.astype(BF16), wd_ref[e], preferred_element_type=F32)
        ptk = jnp.where(dcol == ciota + base, 1.0, 0.0).astype(BF16)
        o_ref[...] += jnp.dot(ptk, y.astype(BF16), preferred_element_type=F32)

    def big(k, carry):
        chunk(k * rp, rp)
        return carry

    lax.fori_loop(0, nbig, big, 0)

    @pl.when((rem > 0) & (rem <= half))
    def _():
        chunk(nbig * rp, half)


def _moe_sparse(x, h, gates, grow, layer, wg_b, wu_b, wd_b, tb, rp):
    n, d = x.shape
    de = wg_b.shape[-1]
    row = lambda i, g: (i, 0)
    kern = functools.partial(_moe_sparse_kernel, rp=rp)
    return pl.pallas_call(
        kern,
        grid=(n // tb, N_GROUPS),
        in_specs=[
            pl.BlockSpec((tb, d), row),
            pl.BlockSpec((tb, d), row),
            pl.BlockSpec((tb, LANES), row),
            pl.BlockSpec((SUBLANES, tb), lambda i, g: (0, i)),
            pl.BlockSpec((None, EXP_PER_GROUP, d, de), lambda i, g: (layer, g, 0, 0)),
            pl.BlockSpec((None, EXP_PER_GROUP, d, de), lambda i, g: (layer, g, 0, 0)),
            pl.BlockSpec((None, EXP_PER_GROUP, de, d), lambda i, g: (layer, g, 0, 0)),
        ],
        out_specs=pl.BlockSpec((tb, d), row),
        out_shape=jax.ShapeDtypeStruct((n, d), F32),
        scratch_shapes=[pltpu.VMEM((tb, tb), BF16), pltpu.VMEM((tb, 1), F32), pltpu.VMEM((SUBLANES, tb), F32),
                        pltpu.SMEM((N_GROUPS,), jnp.int32)],
        compiler_params=_cparams(2),
        name="moe_sparse",
    )(x, h, gates, grow, wg_b, wu_b, wd_b)


def _rope_tables(pos):
    half = ROT_DIM // 2
    inv = jnp.power(ROPE_THETA, -jnp.arange(0, ROT_DIM, 2, dtype=F32) / ROT_DIM)
    ang = pos.astype(F32)[:, None] * inv[None, :]
    cos, sin = jnp.cos(ang), jnp.sin(ang)
    n = pos.shape[0]
    pad = jnp.zeros((n, HEAD_DIM - ROT_DIM), F32)
    zer = jnp.zeros((n, half), F32)
    c = jnp.concatenate([cos, cos, pad + 1.0], axis=1)
    s1 = jnp.concatenate([-sin, zer, pad], axis=1)
    s2 = jnp.concatenate([zer, sin, pad], axis=1)
    rep = LANES // HEAD_DIM
    return jnp.tile(c, (1, rep)), jnp.tile(s1, (1, rep)), jnp.tile(s2, (1, rep))


def kernel(x_prompt, x_sample, cache_k, cache_v, state_conv, page_table, norm1_g, w_in, q_norm_g, k_norm_g,
           lam_q1, lam_k1, lam_q2, lam_k2, subln_g, conv_w, conv_b, conv_ln_g, conv_ln_b, w_out, norm2_g,
           w_router_group, w_router_expert, w_gate, w_up, w_down):
    batch, seq, d = x_prompt.shape
    db = x_sample.shape[0]
    depth = w_in.shape[0]
    n_pages, page = page_table.shape[1], cache_k.shape[2]
    past = n_pages * page
    n = batch * seq

    xp = x_prompt.reshape(n, d)
    xs = x_sample.reshape(db, d)
    tabs_p = _rope_tables(jnp.arange(seq, dtype=jnp.int32))
    tabs_s = _rope_tables(jnp.full((db,), past, dtype=jnp.int32))
    gi = jnp.arange(ATTN_WIDTH, dtype=jnp.int32) // HEAD_DIM
    bd = (gi[:, None] == gi[None, :]).astype(BF16)
    ckt = jnp.transpose(cache_k, (0, 1, 3, 4, 2))
    cvf = cache_v.reshape(depth, cache_v.shape[1], page * N_HEADS, V_DIM)
    state_t = jnp.transpose(state_conv, (0, 2, 1, 3))
    w_in_b, wo_b = w_in.astype(BF16), w_out.astype(BF16)
    wg_b, wu_b, wd_b = w_gate.astype(BF16), w_up.astype(BF16), w_down.astype(BF16)
    kvp = (jnp.zeros((depth, batch, ATTN_WIDTH, seq), F32), jnp.zeros((depth, n * N_HEADS, V_DIM), F32))
    kvs = (jnp.zeros((depth, 1, ATTN_WIDTH, db), F32), jnp.zeros((depth, db * N_HEADS, V_DIM), F32))

    cp_l, cs_l = [], []
    for l in range(depth):
        lam_init = 0.8 - 0.6 * math.exp(-0.3 * l)
        g1 = norm1_g[l][None, :]
        qg = jnp.tile(q_norm_g[l], N_SUB)[None, :]
        kg = jnp.tile(k_norm_g[l], N_SUB)[None, :]
        lam_p = jnp.stack([lam_q1[l], lam_k1[l], lam_q2[l], lam_k2[l]])
        sg = subln_g[l][None, :]
        cwp = jnp.pad(conv_w[l], ((0, HALO - CONV_W), (0, 0)))
        cb, lg, lb = conv_b[l][None, :], conv_ln_g[l][None, :], conv_ln_b[l][None, :]
        g2 = norm2_g[l][None, :]
        wr = jnp.concatenate([w_router_group[l], w_router_expert[l]], axis=1)
        wr_hi = wr.astype(BF16)
        wr_lo = (wr - wr_hi.astype(F32)).astype(BF16)
        nr = wr.shape[1]
        w1 = jnp.zeros((d, LANES), BF16).at[:, :nr].set(wr_hi).at[:, LANES // 2:LANES // 2 + nr].set(wr_lo)
        w2 = jnp.zeros((d, LANES), BF16).at[:, :nr].set(wr_hi)

        q, k_all, v_all, u, kb = _inproj(xp, l, depth, kvp, g1, w_in_b, bd, qg, kg, *tabs_p, tm=512)
        kvp = (k_all, v_all)
        qs, ks_all, vs_all, us, _ = _inproj(xs, l, depth, kvs, g1, w_in_b, bd, qg, kg, *tabs_s, tm=db)
        kvs = (ks_all, vs_all)
        q8 = qs.astype(F32).reshape(db, N_SUB, HEAD_DIM)
        vn8 = jnp.repeat(vs_all[l].reshape(db, N_HEADS, V_DIM), 2, axis=1)
        kn8 = jnp.transpose(ks_all[l, 0].reshape(N_SUB, HEAD_DIM, db), (2, 0, 1))

        att = _prompt_attention(q, kb, v_all, l, lam_p, sg.reshape(V_DIM, 1), batch, seq, 512, lam_init)
        xp, h2, gates, grow, att_s = _mix_prompt(xp, att, u, l, cwp, cb, lg, lb, wo_b, g2, w1, w2, seq, 256,
                                                 page_table, q8, jnp.transpose(q8, (0, 2, 1)), kn8, vn8, lam_p, sg,
                                                 ckt, cvf, lam_init)
        xp = _moe_sparse(xp, h2, gates, grow, l, wg_b, wu_b, wd_b, 1024, 256)
        cp_l.append(u.reshape(batch, seq, CONV_CH)[:, seq - CONV_BUF:])

        att_s = att_s.reshape(db, ATTN_WIDTH).astype(BF16)
        xs, h2, gates, ns = _mix_decode(xs, att_s, us, state_t, l, cwp, cb, lg, lb, wo_b, g2, w1, w2, 64)
        xs = _moe(xs, h2, gates, l, wg_b, wu_b, wd_b, db)
        cs_l.append(jnp.transpose(ns, (1, 0, 2)))

    k_prompt = jnp.transpose(kvp[0].reshape(depth, batch, N_SUB, HEAD_DIM, seq), (0, 1, 4, 2, 3))
    k_sample = jnp.transpose(kvs[0].reshape(depth, 1, N_SUB, HEAD_DIM, db), (0, 4, 1, 2, 3))
    return (xp.reshape(batch, seq, d), xs.reshape(db, 1, d),
            k_prompt, kvp[1].reshape(depth, batch, seq, N_HEADS, V_DIM), jnp.stack(cp_l),
            k_sample, kvs[1].reshape(depth, db, 1, N_HEADS, V_DIM), jnp.stack(cs_l))
```

```python
import functools
import math

import jax
import jax.numpy as jnp
from jax import lax
from jax.experimental import pallas as pl
from jax.experimental.pallas import tpu as pltpu

F32 = jnp.float32
BF16 = jnp.bfloat16

D_MODEL = 1024
HEAD_DIM = 64
N_SUB = 8
N_HEADS = 4
V_DIM = 128
ATTN_WIDTH = 512
CONV_CH = 512
ROT_DIM = 16
ROPE_THETA = 500000.0
CONV_W = 31
CONV_BUF = CONV_W - 1
N_GROUPS = 4
EXP_PER_GROUP = 4
N_EXPERTS = 16
D_EXPERT = 512
EPS = 1e-6
D_IN = 3 * ATTN_WIDTH + 2 * CONV_CH

LANES = 128
SUBLANES = 8
HALO = 32
EXT_TAIL = 16
Q_SCALE = HEAD_DIM ** -0.5 * math.log2(math.e)
QK_AHEAD = 4
GROUP_LANE = N_EXPERTS
VMEM_LIMIT = 56 * 1024 * 1024


def _cparams(n_axes):
    return pltpu.CompilerParams(dimension_semantics=("arbitrary",) * n_axes,
                                vmem_limit_bytes=VMEM_LIMIT)


def _inproj_kernel(x_ref, g1_ref, w_ref, bd_ref, qg_ref, kg_ref, c_ref, s1_ref, s2_ref, *rest):
    q_ref, k_ref, v_ref, u_ref, kb_ref = rest[-5:]
    x = x_ref[...]
    ms = jnp.mean(x * x, axis=-1, keepdims=True)
    h = (x * lax.rsqrt(ms + EPS) * g1_ref[...]).astype(BF16)
    c, s1, s2 = c_ref[...], s1_ref[...], s2_ref[...]
    bd = bd_ref[...]

    def proj(lo, width):
        return jnp.dot(h, w_ref[:, lo:lo + width], preferred_element_type=F32)

    def head_ms(t):
        return jnp.dot((t * t).astype(BF16), bd, preferred_element_type=F32) * (1.0 / HEAD_DIM)

    def norm_rope(t, hms, g):
        tn = t * lax.rsqrt(hms + EPS) * g
        outs = []
        for j in range(ATTN_WIDTH // LANES):
            blk = tn[:, j * LANES:(j + 1) * LANES]
            up = pltpu.roll(blk, LANES - ROT_DIM // 2, 1)
            dn = pltpu.roll(blk, ROT_DIM // 2, 1)
            outs.append(blk * c + up * s1 + dn * s2)
        return outs

    zq = proj(0, ATTN_WIDTH)
    zk = proj(ATTN_WIDTH, ATTN_WIDTH)
    ms_q = head_ms(zq)
    zv = proj(2 * ATTN_WIDTH, ATTN_WIDTH)
    ms_k = head_ms(zk)
    za = proj(3 * ATTN_WIDTH, CONV_CH)
    zg = proj(3 * ATTN_WIDTH + CONV_CH, CONV_CH)
    qs = norm_rope(zq, ms_q, qg_ref[...])
    ks = norm_rope(zk, ms_k, kg_ref[...])
    for j in range(ATTN_WIDTH // LANES):
        sl = slice(j * LANES, (j + 1) * LANES)
        q_ref[:, sl] = (qs[j] * Q_SCALE).astype(BF16)
        k_ref[sl, :] = ks[j].T
        kb_ref[:, sl] = ks[j].astype(BF16)
    tm = x_ref.shape[0]
    for hd in range(N_HEADS):
        v_ref[pl.ds(hd, tm, stride=N_HEADS), :] = zv[:, hd * V_DIM:(hd + 1) * V_DIM]
    u_ref[...] = za * jax.nn.sigmoid(zg)


def _inproj(x, layer, depth, kv_prev, g1, w_in_b, bd, qg, kg, ct, s1t, s2t, tm):
    n = x.shape[0]
    npos = ct.shape[0] // tm
    row = lambda i: (i, 0)
    full = lambda i: (0, 0)
    pos = lambda i: (i % npos, 0)
    lrow = lambda i: (layer, i, 0)
    in_specs = [
        pl.BlockSpec((tm, D_MODEL), row),
        pl.BlockSpec((1, D_MODEL), full),
        pl.BlockSpec((None, D_MODEL, D_IN), lambda i: (layer, 0, 0)),
        pl.BlockSpec((ATTN_WIDTH, ATTN_WIDTH), full),
        pl.BlockSpec((1, ATTN_WIDTH), full),
        pl.BlockSpec((1, ATTN_WIDTH), full),
        pl.BlockSpec((tm, LANES), pos),
        pl.BlockSpec((tm, LANES), pos),
        pl.BlockSpec((tm, LANES), pos),
    ]
    args = [x, g1, w_in_b, bd, qg, kg, ct, s1t, s2t]
    aliases = {len(args): 1, len(args) + 1: 2}
    in_specs += [pl.BlockSpec(memory_space=pl.ANY)] * 2
    args += list(kv_prev)
    return pl.pallas_call(
        _inproj_kernel,
        grid=(n // tm,),
        in_specs=in_specs,
        out_specs=[pl.BlockSpec((tm, ATTN_WIDTH), row),
                   pl.BlockSpec((None, None, ATTN_WIDTH, tm), lambda i: (layer, i // npos, 0, i % npos)),
                   pl.BlockSpec((None, tm * N_HEADS, V_DIM), lrow),
                   pl.BlockSpec((tm, CONV_CH), row),
                   pl.BlockSpec((tm, ATTN_WIDTH), row)],
        out_shape=[
            jax.ShapeDtypeStruct((n, ATTN_WIDTH), BF16),
            jax.ShapeDtypeStruct((depth, n // (npos * tm), ATTN_WIDTH, npos * tm), F32),
            jax.ShapeDtypeStruct((depth, n * N_HEADS, V_DIM), F32),
            jax.ShapeDtypeStruct((n, CONV_CH), F32),
            jax.ShapeDtypeStruct((n, ATTN_WIDTH), BF16),
        ],
        input_output_aliases=aliases,
        compiler_params=_cparams(1),
        name="inproj",
    )(*args)


def _lambda_value(lam_ref, lam_init):
    lp = lam_ref[...]
    t1 = jnp.sum(lp[0:1] * lp[1:2], axis=1, keepdims=True)
    t2 = jnp.sum(lp[2:3] * lp[3:4], axis=1, keepdims=True)
    return jnp.exp(t1) - jnp.exp(t2) + lam_init


def _subln(o, g, lam_init):
    ms = jnp.mean(o * o, axis=-1, keepdims=True)
    return o * lax.rsqrt(ms + EPS) * g * (1.0 - lam_init)


def _pattn_kernel(pt_ref, q_ref, k_ref, v_ref, lam_ref, sgc_ref, q8_ref, qt_ref, kn_ref, vn_ref, sg_ref, ck_hbm, cv_hbm,
                  o_ref, od_ref, vt_scr, qs_scr, m_scr, l_scr, acc_scr, kbuf, vbuf, sem,
                  *, tq, layer, steps, seq0, n_pages, page, per_step, lam_init):
    i = pl.program_id(1)
    step = pl.program_id(0) * pl.num_programs(1) + i
    pk_refs, pv_refs = _sample_pages(pt_ref, step, steps, seq0, ck_hbm, cv_hbm, kbuf, vbuf, sem,
                                     layer=layer, n_pages=n_pages, per_step=per_step)
    nkv = k_ref.shape[0] // tq

    @pl.when(i == 0)
    def _():
        for c in range(nkv):
            for hd in range(N_HEADS):
                rows = v_ref[pl.ds(c * tq * N_HEADS + hd, tq, stride=N_HEADS), :]
                vt_scr[c, hd * V_DIM:(hd + 1) * V_DIM, :] = rows.T.astype(BF16)

    lam = _lambda_value(lam_ref, lam_init)
    lane = lax.broadcasted_iota(jnp.int32, (tq, LANES), 1)
    causal = (lax.broadcasted_iota(jnp.int32, (tq, tq), 0) <= lax.broadcasted_iota(jnp.int32, (tq, tq), 1))

    for sh in range(N_SUB):
        qp = q_ref[:, (sh // 2) * LANES:(sh // 2 + 1) * LANES]
        keep = (lane < HEAD_DIM) if sh % 2 == 0 else (lane >= HEAD_DIM)
        qs_scr[sh] = jnp.where(keep, qp, jnp.zeros_like(qp))
        m_scr[sh] = jnp.full((1, tq), -jnp.inf, F32)
        l_scr[sh] = jnp.zeros((1, tq), F32)
        acc_scr[sh] = jnp.zeros((V_DIM, tq), F32)

    def step(j, masked):
        off = pl.multiple_of(j * tq, tq)
        hq = tq // 2
        chains = [(sh, qh) for sh in range(N_SUB) for qh in range(2)]

        def nkeys(qh):
            return hq if (masked and qh == 0) else tq

        def scores(sh, qh):
            kj = k_ref[pl.ds(off, nkeys(qh)), (sh // 2) * LANES:(sh // 2 + 1) * LANES]
            return lax.dot_general(kj, qs_scr[sh, qh * hq:(qh + 1) * hq, :], (((1,), (1,)), ((), ())),
                                   preferred_element_type=F32)

        pending = [scores(*chains[c]) for c in range(QK_AHEAD)]
        for c, (sh, qh) in enumerate(chains):
            sl = slice((sh // 2) * LANES, (sh // 2 + 1) * LANES)
            cols = slice(qh * hq, (qh + 1) * hq)
            nk = nkeys(qh)
            st = pending.pop(0)
            if c + QK_AHEAD < len(chains):
                pending.append(scores(*chains[c + QK_AHEAD]))
            if masked:
                st = jnp.where(causal[0:nk, cols], st, -jnp.inf)
            m_prev = m_scr[sh, :, cols]
            m_new = jnp.maximum(m_prev, jnp.max(st, axis=0, keepdims=True))
            alpha = jnp.exp2(m_prev - m_new)
            p = jnp.exp2(st - m_new)
            l_scr[sh, :, cols] = alpha * l_scr[sh, :, cols] + jnp.sum(p, axis=0, keepdims=True)
            acc_scr[sh, :, cols] = alpha * acc_scr[sh, :, cols] + jnp.dot(vt_scr[j, sl, 0:nk], p.astype(BF16),
                                                                          preferred_element_type=F32)
            m_scr[sh, :, cols] = m_new

    def body(j, carry):
        step(j, False)
        return carry

    lax.fori_loop(0, i, body, 0)
    step(i, True)

    sgc = sgc_ref[...]
    for hd in range(N_HEADS):
        od = acc_scr[2 * hd] / l_scr[2 * hd] - lam * (acc_scr[2 * hd + 1] / l_scr[2 * hd + 1])
        ms = jnp.mean(od * od, axis=0, keepdims=True)
        on = od * lax.rsqrt(ms + EPS) * sgc * (1.0 - lam_init)
        o_ref[:, hd * LANES:(hd + 1) * LANES] = on.T.astype(BF16)
    _sample_attend(q8_ref, qt_ref, kn_ref, vn_ref, lam, sg_ref[...], pk_refs, pv_refs, od_ref,
                   n_pages=n_pages, page=page, per_step=per_step, lam_init=lam_init)


def _prompt_attention(q, kb, v_all, layer, lam_p, sgc, batch, seq, tq,
                      page_table, seq0, n_seq, q8, qt, kn8, vn8, sg, ckt, cvf, lam_init):
    nq = seq // tq
    n = batch * seq
    steps = batch * nq
    n_pages = page_table.shape[1]
    page = ckt.shape[-1]
    per_step = n_seq // steps
    assert per_step * steps == n_seq
    seq3 = lambda b, i, pt: (b * nq + i, 0, 0)
    const2 = lambda b, i, pt: (0, 0)
    kern = functools.partial(_pattn_kernel, tq=tq, layer=layer, steps=steps, seq0=seq0, n_pages=n_pages, page=page,
                             per_step=per_step, lam_init=lam_init)
    grid_spec = pltpu.PrefetchScalarGridSpec(
        num_scalar_prefetch=1,
        grid=(batch, nq),
        in_specs=[
            pl.BlockSpec((tq, ATTN_WIDTH), lambda b, i, pt: (b * nq + i, 0)),
            pl.BlockSpec((seq, ATTN_WIDTH), lambda b, i, pt: (b, 0)),
            pl.BlockSpec((None, seq * N_HEADS, V_DIM), lambda b, i, pt: (layer, b, 0)),
            pl.BlockSpec((4, HEAD_DIM), const2),
            pl.BlockSpec((V_DIM, 1), const2),
        ] + _sample_specs(per_step, seq3, const2),
        out_specs=[pl.BlockSpec((tq, ATTN_WIDTH), lambda b, i, pt: (b * nq + i, 0)),
                   pl.BlockSpec((per_step, N_HEADS, V_DIM), seq3)],
        scratch_shapes=[
            pltpu.VMEM((nq, ATTN_WIDTH, tq), BF16),
            pltpu.VMEM((N_SUB, tq, LANES), BF16),
            pltpu.VMEM((N_SUB, 1, tq), F32),
            pltpu.VMEM((N_SUB, 1, tq), F32),
            pltpu.VMEM((N_SUB, V_DIM, tq), F32),
        ] + _sample_scratch(per_step, n_pages, page),
    )
    return pl.pallas_call(
        kern,
        grid_spec=grid_spec,
        out_shape=[jax.ShapeDtypeStruct((n, ATTN_WIDTH), BF16), jax.ShapeDtypeStruct((n_seq, N_HEADS, V_DIM), F32)],
        compiler_params=_cparams(2),
        name="prompt_attn",
    )(page_table, q, kb, v_all, lam_p, sgc, q8, qt, kn8, vn8, sg, ckt, cvf)


def _decode_scores(q8, qt, kn, k_refs, page):
    qcols = [jnp.broadcast_to(qt[:, h:h + 1], (HEAD_DIM, page)) for h in range(N_SUB)]
    s = jnp.concatenate(
        [jnp.concatenate([jnp.sum(k_ref[h] * qcols[h], axis=0, keepdims=True) for h in range(N_SUB)], axis=0)
         for k_ref in k_refs], axis=1)
    s_self = jnp.sum(q8 * kn, axis=1, keepdims=True)
    return s, s_self


def _decode_probs(s, s_self, lam):
    m = jnp.maximum(jnp.max(s, axis=1, keepdims=True), s_self)
    pexp = jnp.exp2(s - m)
    pself = jnp.exp2(s_self - m)
    inv = 1.0 / (jnp.sum(pexp, axis=1, keepdims=True) + pself)
    sub = lax.broadcasted_iota(jnp.int32, (N_SUB, 1), 0)
    coef = jnp.where((sub & 1) == 0, 1.0, -lam) * inv
    a8 = pexp * coef
    a8 = (a8 + pltpu.roll(a8, N_SUB - 1, 0)).astype(BF16)
    as8 = jnp.broadcast_to(pself * coef, (N_SUB, V_DIM))
    return a8, as8 + pltpu.roll(as8, N_SUB - 1, 0)


def _decode_values(a8, as8, vn, sg, v_refs, page, lam_init):
    o_self = as8 * vn
    rows = []
    for hd in range(N_HEADS):
        v_hd = jnp.concatenate([v_ref[pl.ds(hd, page, stride=N_HEADS), :].astype(BF16) for v_ref in v_refs], axis=0)
        acc = jnp.dot(a8, v_hd, preferred_element_type=F32)
        rows.append(acc[2 * hd:2 * hd + 1] + o_self[2 * hd:2 * hd + 1])
    o = jnp.concatenate(rows, axis=0)
    return _subln(o, sg, lam_init)


def _sample_pages(pt_ref, step, steps, seq0, ck_hbm, cv_hbm, kbuf, vbuf, sem, *, layer, n_pages, per_step):
    nkv = per_step * n_pages

    def page_copy(slot, j, phys):
        return (pltpu.make_async_copy(ck_hbm.at[layer, phys], kbuf.at[slot, j], sem.at[slot]),
                pltpu.make_async_copy(cv_hbm.at[layer, phys], vbuf.at[slot, j], sem.at[slot]))

    def start_pages(st, slot):
        for j in range(nkv):
            for cp in page_copy(slot, j, pt_ref[seq0 + st * per_step + j // n_pages, j % n_pages]):
                cp.start()

    slot = step % 2

    @pl.when(step == 0)
    def _():
        start_pages(0, 0)

    for j in range(nkv):
        for cp in page_copy(slot, j, 0):
            cp.wait()

    @pl.when(step + 1 < steps)
    def _():
        start_pages(step + 1, 1 - slot)

    return [kbuf.at[slot, j] for j in range(nkv)], [vbuf.at[slot, j] for j in range(nkv)]


def _sample_attend(q8_ref, qt_ref, kn_ref, vn_ref, lam, sg, k_refs, v_refs, od_ref, *, n_pages, page, per_step, lam_init):
    pages = [slice(sq * n_pages, (sq + 1) * n_pages) for sq in range(per_step)]
    scores = [_decode_scores(q8_ref[sq], qt_ref[sq], kn_ref[sq], k_refs[pages[sq]], page) for sq in range(per_step)]
    probs = [_decode_probs(*scores[sq], lam) for sq in range(per_step)]
    for sq in range(per_step):
        od_ref[sq] = _decode_values(*probs[sq], vn_ref[sq], sg, v_refs[pages[sq]], page, lam_init)


def _sample_scratch(per_step, n_pages, page):
    return [pltpu.VMEM((2, per_step * n_pages, N_SUB, HEAD_DIM, page), F32),
            pltpu.VMEM((2, per_step * n_pages, page * N_HEADS, V_DIM), F32),
            pltpu.SemaphoreType.DMA((2,))]


def _sample_specs(per_step, seq3, const2):
    return [pl.BlockSpec((per_step, N_SUB, HEAD_DIM), seq3), pl.BlockSpec((per_step, HEAD_DIM, N_SUB), seq3),
            pl.BlockSpec((per_step, N_SUB, HEAD_DIM), seq3), pl.BlockSpec((per_step, N_SUB, V_DIM), seq3),
            pl.BlockSpec((1, V_DIM), const2),
            pl.BlockSpec(memory_space=pl.ANY), pl.BlockSpec(memory_space=pl.ANY)]


def _conv_post(y, lg_ref, lb_ref):
    mu = jnp.mean(y, axis=-1, keepdims=True)
    yc = y - mu
    var = jnp.mean(yc * yc, axis=-1, keepdims=True)
    yn = yc * lax.rsqrt(var + EPS) * lg_ref[...] + lb_ref[...]
    return yn * jax.nn.sigmoid(yn)


def _mix_tail(x, att, cnv, wo_ref, g2_ref, w1_ref, w2_ref, xo_ref, h_ref, gate_ref, grow_ref=None):
    y = (x + jnp.dot(att, wo_ref[0:ATTN_WIDTH, :], preferred_element_type=F32)
         + jnp.dot(cnv.astype(BF16), wo_ref[ATTN_WIDTH:, :], preferred_element_type=F32))
    xo_ref[...] = y
    ms = jnp.mean(y * y, axis=-1, keepdims=True)
    h = y * lax.rsqrt(ms + EPS) * g2_ref[...]
    h_hi = h.astype(BF16)
    h_lo = (h - h_hi.astype(F32)).astype(BF16)
    h_ref[...] = h_hi
    r = jnp.dot(h_hi, w1_ref[...], preferred_element_type=F32)
    lg = r + pltpu.roll(r, LANES // 2, 1) + jnp.dot(h_lo, w2_ref[...], preferred_element_type=F32)
    tm = lg.shape[0]
    lane = lax.broadcasted_iota(jnp.int32, (tm, LANES), 1)
    lanef = lane.astype(F32)
    big = float(LANES)
    neg = -jnp.inf
    gmask = lane < N_GROUPS
    gl = jnp.where(gmask, lg, neg)
    gmax = jnp.max(gl, axis=1, keepdims=True)
    gsum = jnp.sum(jnp.where(gmask, jnp.exp(gl - gmax), 0.0), axis=1, keepdims=True)
    g_w = 1.0 / gsum
    g_idx = jnp.min(jnp.where(gl == gmax, lanef, big), axis=1, keepdims=True)
    lo = N_GROUPS + EXP_PER_GROUP * g_idx
    emask = (lanef >= lo) & (lanef < lo + EXP_PER_GROUP)
    ev = jnp.where(emask, lg, neg)
    v1 = jnp.max(ev, axis=1, keepdims=True)
    i1 = jnp.min(jnp.where(ev == v1, lanef, big), axis=1, keepdims=True)
    ev2 = jnp.where(lanef == i1, neg, ev)
    v2 = jnp.max(ev2, axis=1, keepdims=True)
    i2 = jnp.min(jnp.where(ev2 == v2, lanef, big), axis=1, keepdims=True)
    e2 = jnp.exp(v2 - v1)
    den = 1.0 / (1.0 + e2)
    w1 = den * g_w
    w2 = e2 * den * g_w
    gates = jnp.where(lanef == i1, w1, jnp.where(lanef == i2, w2, 0.0))
    gates = pltpu.roll(gates, LANES - N_GROUPS, 1)
    gate_ref[...] = jnp.where(lane == GROUP_LANE, g_idx, gates)
    if grow_ref is not None:
        grow_ref[...] = jnp.transpose(jnp.broadcast_to(g_idx, (tm, LANES)))[0:SUBLANES, :]


def _mix_prompt_kernel(pt_ref, x_ref, att_ref, u_ref, uh_ref, cw_ref, cb_ref, lg_ref, lb_ref, wo_ref, g2_ref,
                       w1_ref, w2_ref, lam_ref, q8_ref, qt_ref, kn_ref, vn_ref, sg_ref, ck_hbm, cv_hbm,
                       xo_ref, h_ref, gate_ref, grow_ref, od_ref, ext_scr, kbuf, vbuf, sem,
                       *, tm, tiles_per_seq, layer, steps, seq0, n_pages, page, per_step, lam_init):
    i = pl.program_id(0)
    k_refs, v_refs = _sample_pages(pt_ref, i, steps, seq0, ck_hbm, cv_hbm, kbuf, vbuf, sem,
                                   layer=layer, n_pages=n_pages, per_step=per_step)
    first = (i % tiles_per_seq) == 0
    ext_scr[0:HALO, :] = jnp.where(first, 0.0, uh_ref[...])
    ext_scr[HALO:HALO + tm, :] = u_ref[...]
    ext_scr[HALO + tm:, :] = jnp.zeros((EXT_TAIL, CONV_CH), F32)
    lead = HALO - CONV_BUF
    acc = jnp.zeros((tm, CONV_CH), F32) + cb_ref[...]
    for r in range(SUBLANES):
        z = None
        for a in range((CONV_W + lead + SUBLANES - 1) // SUBLANES):
            j = SUBLANES * a + r - lead
            if 0 <= j < CONV_W:
                term = cw_ref[j:j + 1, :] * ext_scr[SUBLANES * a:SUBLANES * a + tm + SUBLANES, :]
                z = term if z is None else z + term
        acc = acc + z[r:r + tm]
    cnv = _conv_post(acc, lg_ref, lb_ref)
    _mix_tail(x_ref[...], att_ref[...], cnv, wo_ref, g2_ref, w1_ref, w2_ref, xo_ref, h_ref, gate_ref, grow_ref)
    _sample_attend(q8_ref, qt_ref, kn_ref, vn_ref, _lambda_value(lam_ref, lam_init), sg_ref[...], k_refs, v_refs, od_ref,
                   n_pages=n_pages, page=page, per_step=per_step, lam_init=lam_init)


def _mix_decode_kernel(x_ref, att_ref, u_ref, st_ref, cw_ref, cb_ref, lg_ref, lb_ref, wo_ref, g2_ref,
                       w1_ref, w2_ref, xo_ref, h_ref, gate_ref, ns_ref):
    u = u_ref[...]
    acc = cb_ref[...] + cw_ref[CONV_BUF:CONV_BUF + 1, :] * u
    for j in range(CONV_BUF):
        acc = acc + cw_ref[j:j + 1, :] * st_ref[j]
    for j in range(CONV_BUF - 1):
        ns_ref[j] = st_ref[j + 1]
    ns_ref[CONV_BUF - 1] = u
    cnv = _conv_post(acc, lg_ref, lb_ref)
    _mix_tail(x_ref[...], att_ref[...], cnv, wo_ref, g2_ref, w1_ref, w2_ref, xo_ref, h_ref, gate_ref)


def _mix_common_specs(tm, layer):
    row = lambda i, *_: (i, 0)
    full = lambda i, *_: (0, 0)
    tail_in = [
        pl.BlockSpec((HALO, CONV_CH), full),
        pl.BlockSpec((1, CONV_CH), full),
        pl.BlockSpec((1, CONV_CH), full),
        pl.BlockSpec((1, CONV_CH), full),
        pl.BlockSpec((None, D_MODEL, D_MODEL), lambda i, *_: (layer, 0, 0)),
        pl.BlockSpec((1, D_MODEL), full),
        pl.BlockSpec((D_MODEL, LANES), full),
        pl.BlockSpec((D_MODEL, LANES), full),
    ]
    out_specs = [pl.BlockSpec((tm, D_MODEL), row), pl.BlockSpec((tm, D_MODEL), row),
                 pl.BlockSpec((tm, LANES), row)]
    return row, tail_in, out_specs


def _mix_out_shape(n):
    return [jax.ShapeDtypeStruct((n, D_MODEL), F32), jax.ShapeDtypeStruct((n, D_MODEL), BF16),
            jax.ShapeDtypeStruct((n, LANES), F32)]


def _mix_prompt(x, att, u, layer, cwp, cb, lg, lb, wo_b, g2, w1, w2, seq, tm,
                page_table, seq0, n_seq, q8, qt, kn8, vn8, lam_p, sg, ckt, cvf, lam_init):
    n = x.shape[0]
    steps = n // tm
    n_pages = page_table.shape[1]
    page = ckt.shape[-1]
    per_step = n_seq // steps
    assert per_step * steps == n_seq
    row, tail_in, out_specs = _mix_common_specs(tm, layer)
    halo_map = lambda i, pt: (jnp.maximum(i * (tm // HALO) - 1, 0), 0)
    seq3 = lambda i, pt: (i, 0, 0)
    const2 = lambda i, pt: (0, 0)
    in_specs = [pl.BlockSpec((tm, D_MODEL), row), pl.BlockSpec((tm, ATTN_WIDTH), row),
                pl.BlockSpec((tm, CONV_CH), row), pl.BlockSpec((HALO, CONV_CH), halo_map)] + tail_in
    in_specs += [pl.BlockSpec((4, HEAD_DIM), const2)] + _sample_specs(per_step, seq3, const2)
    kern = functools.partial(_mix_prompt_kernel, tm=tm, tiles_per_seq=seq // tm, layer=layer, steps=steps, seq0=seq0,
                             n_pages=n_pages, page=page, per_step=per_step, lam_init=lam_init)
    grid_spec = pltpu.PrefetchScalarGridSpec(
        num_scalar_prefetch=1,
        grid=(steps,),
        in_specs=in_specs,
        out_specs=out_specs + [pl.BlockSpec((SUBLANES, tm), lambda i, pt: (0, i)),
                               pl.BlockSpec((per_step, N_HEADS, V_DIM), seq3)],
        scratch_shapes=[pltpu.VMEM((tm + HALO + EXT_TAIL, CONV_CH), F32)] + _sample_scratch(per_step, n_pages, page),
    )
    return pl.pallas_call(
        kern,
        grid_spec=grid_spec,
        out_shape=_mix_out_shape(n) + [jax.ShapeDtypeStruct((SUBLANES, n), F32),
                                       jax.ShapeDtypeStruct((n_seq, N_HEADS, V_DIM), F32)],
        compiler_params=_cparams(1),
        name="mix_prompt",
    )(page_table, x, att, u, u, cwp, cb, lg, lb, wo_b, g2, w1, w2, lam_p, q8, qt, kn8, vn8, sg, ckt, cvf)


def _mix_decode(x, att, u, state_t, layer, cwp, cb, lg, lb, wo_b, g2, w1, w2, tm):
    n = x.shape[0]
    row, tail_in, out_specs = _mix_common_specs(tm, layer)
    return pl.pallas_call(
        _mix_decode_kernel,
        grid=(n // tm,),
        in_specs=[pl.BlockSpec((tm, D_MODEL), row), pl.BlockSpec((tm, ATTN_WIDTH), row),
                  pl.BlockSpec((tm, CONV_CH), row),
                  pl.BlockSpec((None, CONV_BUF, tm, CONV_CH), lambda i: (layer, 0, i, 0))] + tail_in,
        out_specs=out_specs + [pl.BlockSpec((CONV_BUF, tm, CONV_CH), lambda i: (0, i, 0))],
        out_shape=_mix_out_shape(n) + [jax.ShapeDtypeStruct((CONV_BUF, n, CONV_CH), F32)],
        compiler_params=_cparams(1),
        name="mix_decode",
    )(x, att, u, state_t, cwp, cb, lg, lb, wo_b, g2, w1, w2)


def _moe_kernel(x_ref, h_ref, gate_ref, wg_ref, wu_ref, wd_ref, o_ref):
    e = pl.program_id(1)

    @pl.when(e == 0)
    def _():
        o_ref[...] = x_ref[...]

    h = h_ref[...]
    hg = jnp.dot(h, wg_ref[...], preferred_element_type=F32)
    hu = jnp.dot(h, wu_ref[...], preferred_element_type=F32)
    gates = gate_ref[...]
    lane = lax.broadcasted_iota(jnp.int32, gates.shape, 1)
    ge = jnp.sum(jnp.where(lane == e, gates, 0.0), axis=1, keepdims=True)
    act = (hg * jax.nn.sigmoid(hg)) * hu * ge
    o_ref[...] += jnp.dot(act.astype(BF16), wd_ref[...], preferred_element_type=F32)


def _moe(x, h, gates, layer, wg_b, wu_b, wd_b, tm):
    n = x.shape[0]
    row = lambda i, e: (i, 0)
    return pl.pallas_call(
        _moe_kernel,
        grid=(n // tm, N_EXPERTS),
        in_specs=[
            pl.BlockSpec((tm, D_MODEL), row),
            pl.BlockSpec((tm, D_MODEL), row),
            pl.BlockSpec((tm, LANES), row),
            pl.BlockSpec((None, None, D_MODEL, D_EXPERT), lambda i, e: (layer, e, 0, 0)),
            pl.BlockSpec((None, None, D_MODEL, D_EXPERT), lambda i, e: (layer, e, 0, 0)),
            pl.BlockSpec((None, None, D_EXPERT, D_MODEL), lambda i, e: (layer, e, 0, 0)),
        ],
        out_specs=pl.BlockSpec((tm, D_MODEL), row),
        out_shape=jax.ShapeDtypeStruct((n, D_MODEL), F32),
        compiler_params=_cparams(2),
        name="moe",
    )(x, h, gates, wg_b, wu_b, wd_b)


def _moe_sparse_kernel(x_ref, h_ref, gate_ref, grow_ref, wg_ref, wu_ref, wd_ref, o_ref, lt_scr, dcol_scr, drow_scr,
                       cnt_scr, *, rp):
    i = pl.program_id(0)
    g = pl.program_id(1)
    tb = h_ref.shape[0]

    @pl.when((i == 0) & (g == 0))
    def _():
        r = lax.broadcasted_iota(jnp.int32, (tb, tb), 0)
        c = lax.broadcasted_iota(jnp.int32, (tb, tb), 1)
        lt_scr[...] = jnp.where(c < r, 1.0, 0.0).astype(BF16)

    @pl.when(g == 0)
    def _():
        o_ref[...] = x_ref[...]

    gf = g.astype(F32)
    gates = gate_ref[...]
    lane = lax.broadcasted_iota(jnp.int32, gates.shape, 1)
    gcol = jnp.sum(jnp.where(lane == GROUP_LANE, gates, 0.0), axis=1, keepdims=True)
    grow = grow_ref[0:1, :]

    @pl.when(g == 0)
    def _():
        lt = lt_scr[...]
        oh_c = jnp.where(gcol == lane.astype(F32), 1.0, 0.0)
        rk_c = jnp.dot(lt, oh_c.astype(BF16), preferred_element_type=F32)
        dcol_scr[...] = jnp.sum(oh_c * rk_c, axis=1, keepdims=True)
        sub = lax.broadcasted_iota(jnp.int32, (SUBLANES, tb), 0).astype(F32)
        oh_r = jnp.where(grow == sub, 1.0, 0.0)
        rk_r = lax.dot_general(oh_r.astype(BF16), lt, (((1,), (1,)), ((), ())), preferred_element_type=F32)
        drow_scr[...] = jnp.broadcast_to(jnp.sum(oh_r * rk_r, axis=0, keepdims=True), (SUBLANES, tb))
        for l in range(N_GROUPS):
            cnt_scr[l] = jnp.sum(oh_r[l:l + 1, :]).astype(jnp.int32)

    dcol = jnp.where(gcol == gf, dcol_scr[...], -1.0)
    drow = jnp.where(grow == gf, drow_scr[0:1, :], -1.0)
    cnt = cnt_scr[g]
    half = rp // 2
    rem = cnt % rp
    nbig = cnt // rp + (rem > half).astype(jnp.int32)

    g_hi = gates.astype(BF16)
    g_lo = (gates - g_hi.astype(F32)).astype(BF16)
    h = h_ref[...]

    def chunk(first_slot, rows):
        base = first_slot.astype(F32)
        lane_r = lax.broadcasted_iota(jnp.int32, (rows, LANES), 1)
        riota = lax.broadcasted_iota(jnp.int32, (rows, 1), 0).astype(F32)
        ciota = lax.broadcasted_iota(jnp.int32, (1, rows), 1).astype(F32)
        pk = jnp.where(drow == riota + base, 1.0, 0.0).astype(BF16)
        xs = jnp.dot(pk, h, preferred_element_type=F32).astype(BF16)
        gs = (jnp.dot(pk, g_hi, preferred_element_type=F32)
              + jnp.dot(pk, g_lo, preferred_element_type=F32))
        y = jnp.zeros((rows, o_ref.shape[1]), F32)
        for e in range(EXP_PER_GROUP):
            hg = jnp.dot(xs, wg_ref[e], preferred_element_type=F32)
            hu = jnp.dot(xs, wu_ref[e], preferred_element_type=F32)
            ge = jnp.sum(jnp.where(lane_r == g * EXP_PER_GROUP + e, gs, 0.0), axis=1, keepdims=True)
            act = (hg * jax.nn.sigmoid(hg)) * hu * ge
            y = y + jnp.dot(act.astype(BF16), wd_ref[e], preferred_element_type=F32)
        ptk = jnp.where(dcol == ciota + base, 1.0, 0.0).astype(BF16)
        o_ref[...] += jnp.dot(ptk, y.astype(BF16), preferred_element_type=F32)

    def big(k, carry):
        chunk(k * rp, rp)
        return carry

    lax.fori_loop(0, nbig, big, 0)

    @pl.when((rem > 0) & (rem <= half))
    def _():
        chunk(nbig * rp, half)


def _moe_sparse(x, h, gates, grow, layer, wg_b, wu_b, wd_b, tb, rp):
    n, d = x.shape
    de = wg_b.shape[-1]
    row = lambda i, g: (i, 0)
    kern = functools.partial(_moe_sparse_kernel, rp=rp)
    return pl.pallas_call(
        kern,
        grid=(n // tb, N_GROUPS),
        in_specs=[
            pl.BlockSpec((tb, d), row),
            pl.BlockSpec((tb, d), row),
            pl.BlockSpec((tb, LANES), row),
            pl.BlockSpec((SUBLANES, tb), lambda i, g: (0, i)),
            pl.BlockSpec((None, EXP_PER_GROUP, d, de), lambda i, g: (layer, g, 0, 0)),
            pl.BlockSpec((None, EXP_PER_GROUP, d, de), lambda i, g: (layer, g, 0, 0)),
            pl.BlockSpec((None, EXP_PER_GROUP, de, d), lambda i, g: (layer, g, 0, 0)),
        ],
        out_specs=pl.BlockSpec((tb, d), row),
        out_shape=jax.ShapeDtypeStruct((n, d), F32),
        scratch_shapes=[pltpu.VMEM((tb, tb), BF16), pltpu.VMEM((tb, 1), F32), pltpu.VMEM((SUBLANES, tb), F32),
                        pltpu.SMEM((N_GROUPS,), jnp.int32)],
        compiler_params=_cparams(2),
        name="moe_sparse",
    )(x, h, gates, grow, wg_b, wu_b, wd_b)


def _rope_tables(pos):
    half = ROT_DIM // 2
    inv = jnp.power(ROPE_THETA, -jnp.arange(0, ROT_DIM, 2, dtype=F32) / ROT_DIM)
    ang = pos.astype(F32)[:, None] * inv[None, :]
    cos, sin = jnp.cos(ang), jnp.sin(ang)
    n = pos.shape[0]
    pad = jnp.zeros((n, HEAD_DIM - ROT_DIM), F32)
    zer = jnp.zeros((n, half), F32)
    c = jnp.concatenate([cos, cos, pad + 1.0], axis=1)
    s1 = jnp.concatenate([-sin, zer, pad], axis=1)
    s2 = jnp.concatenate([zer, sin, pad], axis=1)
    rep = LANES // HEAD_DIM
    return jnp.tile(c, (1, rep)), jnp.tile(s1, (1, rep)), jnp.tile(s2, (1, rep))


def kernel(x_prompt, x_sample, cache_k, cache_v, state_conv, page_table, norm1_g, w_in, q_norm_g, k_norm_g,
           lam_q1, lam_k1, lam_q2, lam_k2, subln_g, conv_w, conv_b, conv_ln_g, conv_ln_b, w_out, norm2_g,
           w_router_group, w_router_expert, w_gate, w_up, w_down):
    batch, seq, d = x_prompt.shape
    db = x_sample.shape[0]
    depth = w_in.shape[0]
    n_pages, page = page_table.shape[1], cache_k.shape[2]
    past = n_pages * page
    n = batch * seq

    xp = x_prompt.reshape(n, d)
    xs = x_sample.reshape(db, d)
    tabs_p = _rope_tables(jnp.arange(seq, dtype=jnp.int32))
    tabs_s = _rope_tables(jnp.full((db,), past, dtype=jnp.int32))
    gi = jnp.arange(ATTN_WIDTH, dtype=jnp.int32) // HEAD_DIM
    bd = (gi[:, None] == gi[None, :]).astype(BF16)
    ckt = jnp.transpose(cache_k, (0, 1, 3, 4, 2))
    cvf = cache_v.reshape(depth, cache_v.shape[1], page * N_HEADS, V_DIM)
    state_t = jnp.transpose(state_conv, (0, 2, 1, 3))
    w_in_b, wo_b = w_in.astype(BF16), w_out.astype(BF16)
    wg_b, wu_b, wd_b = w_gate.astype(BF16), w_up.astype(BF16), w_down.astype(BF16)
    kvp = (jnp.zeros((depth, batch, ATTN_WIDTH, seq), F32), jnp.zeros((depth, n * N_HEADS, V_DIM), F32))
    kvs = (jnp.zeros((depth, 1, ATTN_WIDTH, db), F32), jnp.zeros((depth, db * N_HEADS, V_DIM), F32))

    cp_l, cs_l = [], []
    for l in range(depth):
        lam_init = 0.8 - 0.6 * math.exp(-0.3 * l)
        g1 = norm1_g[l][None, :]
        qg = jnp.tile(q_norm_g[l], N_SUB)[None, :]
        kg = jnp.tile(k_norm_g[l], N_SUB)[None, :]
        lam_p = jnp.stack([lam_q1[l], lam_k1[l], lam_q2[l], lam_k2[l]])
        sg = subln_g[l][None, :]
        cwp = jnp.pad(conv_w[l], ((0, HALO - CONV_W), (0, 0)))
        cb, lg, lb = conv_b[l][None, :], conv_ln_g[l][None, :], conv_ln_b[l][None, :]
        g2 = norm2_g[l][None, :]
        wr = jnp.concatenate([w_router_group[l], w_router_expert[l]], axis=1)
        wr_hi = wr.astype(BF16)
        wr_lo = (wr - wr_hi.astype(F32)).astype(BF16)
        nr = wr.shape[1]
        w1 = jnp.zeros((d, LANES), BF16).at[:, :nr].set(wr_hi).at[:, LANES // 2:LANES // 2 + nr].set(wr_lo)
        w2 = jnp.zeros((d, LANES), BF16).at[:, :nr].set(wr_hi)

        q, k_all, v_all, u, kb = _inproj(xp, l, depth, kvp, g1, w_in_b, bd, qg, kg, *tabs_p, tm=512)
        kvp = (k_all, v_all)
        qs, ks_all, vs_all, us, _ = _inproj(xs, l, depth, kvs, g1, w_in_b, bd, qg, kg, *tabs_s, tm=db)
        kvs = (ks_all, vs_all)
        q8 = qs.astype(F32).reshape(db, N_SUB, HEAD_DIM)
        vn8 = jnp.repeat(vs_all[l].reshape(db, N_HEADS, V_DIM), 2, axis=1)
        kn8 = jnp.transpose(ks_all[l, 0].reshape(N_SUB, HEAD_DIM, db), (2, 0, 1))

        qt = jnp.transpose(q8, (0, 2, 1))
        tm_mix, tq = 256, 512
        n_mix = n // tm_mix
        att, att_s2 = _prompt_attention(q, kb, v_all, l, lam_p, sg.reshape(V_DIM, 1), batch, seq, tq, page_table, n_mix,
                                        db - n_mix, q8[n_mix:], qt[n_mix:], kn8[n_mix:], vn8[n_mix:], sg, ckt, cvf,
                                        lam_init)
        xp, h2, gates, grow, att_s1 = _mix_prompt(xp, att, u, l, cwp, cb, lg, lb, wo_b, g2, w1, w2, seq, tm_mix,
                                                  page_table, 0, n_mix, q8[:n_mix], qt[:n_mix], kn8[:n_mix],
                                                  vn8[:n_mix], lam_p, sg, ckt, cvf, lam_init)
        att_s = jnp.concatenate([att_s1, att_s2], axis=0)
        xp = _moe_sparse(xp, h2, gates, grow, l, wg_b, wu_b, wd_b, 1024, 256)
        cp_l.append(u.reshape(batch, seq, CONV_CH)[:, seq - CONV_BUF:])

        att_s = att_s.reshape(db, ATTN_WIDTH).astype(BF16)
        xs, h2, gates, ns = _mix_decode(xs, att_s, us, state_t, l, cwp, cb, lg, lb, wo_b, g2, w1, w2, 64)
        xs = _moe(xs, h2, gates, l, wg_b, wu_b, wd_b, db)
        cs_l.append(jnp.transpose(ns, (1, 0, 2)))

    k_prompt = jnp.transpose(kvp[0].reshape(depth, batch, N_SUB, HEAD_DIM, seq), (0, 1, 4, 2, 3))
    k_sample = jnp.transpose(kvs[0].reshape(depth, 1, N_SUB, HEAD_DIM, db), (0, 4, 1, 2, 3))
    return (xp.reshape(batch, seq, d), xs.reshape(db, 1, d),
            k_prompt, kvp[1].reshape(depth, batch, seq, N_HEADS, V_DIM), jnp.stack(cp_l),
            k_sample, kvs[1].reshape(depth, db, 1, N_HEADS, V_DIM), jnp.stack(cs_l))
```

```python
import functools
import math

import jax
import jax.numpy as jnp
from jax import lax
from jax.experimental import pallas as pl
from jax.experimental.pallas import tpu as pltpu

F32 = jnp.float32
BF16 = jnp.bfloat16

D_MODEL = 1024
HEAD_DIM = 64
N_SUB = 8
N_HEADS = 4
V_DIM = 128
ATTN_WIDTH = 512
CONV_CH = 512
ROT_DIM = 16
ROPE_THETA = 500000.0
CONV_W = 31
CONV_BUF = CONV_W - 1
N_GROUPS = 4
EXP_PER_GROUP = 4
N_EXPERTS = 16
D_EXPERT = 512
EPS = 1e-6
D_IN = 3 * ATTN_WIDTH + 2 * CONV_CH

LANES = 128
SUBLANES = 8
HALO = 32
EXT_TAIL = 16
Q_SCALE = HEAD_DIM ** -0.5 * math.log2(math.e)
QK_AHEAD = 4
GROUP_LANE = N_EXPERTS
VMEM_LIMIT = 56 * 1024 * 1024

TM_PROJ = 512
TQ_ATTN = 512
TM_MIX = 256
TM_MIX_SAMPLE = 64
TB_MOE = 1024
RP_MOE = 256


def _cparams(n_axes):
    return pltpu.CompilerParams(dimension_semantics=("arbitrary",) * n_axes,
                                vmem_limit_bytes=VMEM_LIMIT)


def _inproj_kernel(x_ref, g1_ref, w_ref, bd_ref, qg_ref, kg_ref, c_ref, s1_ref, s2_ref, *rest):
    q_ref, k_ref, v_ref, u_ref, kb_ref = rest[-5:]
    x = x_ref[...]
    ms = jnp.mean(x * x, axis=-1, keepdims=True)
    h = (x * lax.rsqrt(ms + EPS) * g1_ref[...]).astype(BF16)
    c, s1, s2 = c_ref[...], s1_ref[...], s2_ref[...]
    bd = bd_ref[...]

    def proj(lo, width):
        return jnp.dot(h, w_ref[:, lo:lo + width], preferred_element_type=F32)

    def head_ms(t):
        return jnp.dot((t * t).astype(BF16), bd, preferred_element_type=F32) * (1.0 / HEAD_DIM)

    def norm_rope(t, hms, g):
        tn = t * lax.rsqrt(hms + EPS) * g
        outs = []
        for j in range(ATTN_WIDTH // LANES):
            blk = tn[:, j * LANES:(j + 1) * LANES]
            up = pltpu.roll(blk, LANES - ROT_DIM // 2, 1)
            dn = pltpu.roll(blk, ROT_DIM // 2, 1)
            outs.append(blk * c + up * s1 + dn * s2)
        return outs

    zq = proj(0, ATTN_WIDTH)
    zk = proj(ATTN_WIDTH, ATTN_WIDTH)
    ms_q = head_ms(zq)
    zv = proj(2 * ATTN_WIDTH, ATTN_WIDTH)
    ms_k = head_ms(zk)
    za = proj(3 * ATTN_WIDTH, CONV_CH)
    zg = proj(3 * ATTN_WIDTH + CONV_CH, CONV_CH)
    qs = norm_rope(zq, ms_q, qg_ref[...])
    ks = norm_rope(zk, ms_k, kg_ref[...])
    for j in range(ATTN_WIDTH // LANES):
        sl = slice(j * LANES, (j + 1) * LANES)
        q_ref[:, sl] = (qs[j] * Q_SCALE).astype(BF16)
        k_ref[sl, :] = ks[j].T
        kb_ref[:, sl] = ks[j].astype(BF16)
    tm = x_ref.shape[0]
    for hd in range(N_HEADS):
        v_ref[pl.ds(hd, tm, stride=N_HEADS), :] = zv[:, hd * V_DIM:(hd + 1) * V_DIM]
    u_ref[...] = za * jax.nn.sigmoid(zg)


def _inproj(x, layer, depth, kv_prev, g1, w_in_b, bd, qg, kg, ct, s1t, s2t, tm):
    n = x.shape[0]
    npos = ct.shape[0] // tm
    row = lambda i: (i, 0)
    full = lambda i: (0, 0)
    pos = lambda i: (i % npos, 0)
    lrow = lambda i: (layer, i, 0)
    in_specs = [
        pl.BlockSpec((tm, D_MODEL), row),
        pl.BlockSpec((1, D_MODEL), full),
        pl.BlockSpec((None, D_MODEL, D_IN), lambda i: (layer, 0, 0)),
        pl.BlockSpec((ATTN_WIDTH, ATTN_WIDTH), full),
        pl.BlockSpec((1, ATTN_WIDTH), full),
        pl.BlockSpec((1, ATTN_WIDTH), full),
        pl.BlockSpec((tm, LANES), pos),
        pl.BlockSpec((tm, LANES), pos),
        pl.BlockSpec((tm, LANES), pos),
    ]
    args = [x, g1, w_in_b, bd, qg, kg, ct, s1t, s2t]
    aliases = {len(args): 1, len(args) + 1: 2}
    in_specs += [pl.BlockSpec(memory_space=pl.ANY)] * 2
    args += list(kv_prev)
    return pl.pallas_call(
        _inproj_kernel,
        grid=(n // tm,),
        in_specs=in_specs,
        out_specs=[pl.BlockSpec((tm, ATTN_WIDTH), row),
                   pl.BlockSpec((None, None, ATTN_WIDTH, tm), lambda i: (layer, i // npos, 0, i % npos)),
                   pl.BlockSpec((None, tm * N_HEADS, V_DIM), lrow),
                   pl.BlockSpec((tm, CONV_CH), row),
                   pl.BlockSpec((tm, ATTN_WIDTH), row)],
        out_shape=[
            jax.ShapeDtypeStruct((n, ATTN_WIDTH), BF16),
            jax.ShapeDtypeStruct((depth, n // (npos * tm), ATTN_WIDTH, npos * tm), F32),
            jax.ShapeDtypeStruct((depth, n * N_HEADS, V_DIM), F32),
            jax.ShapeDtypeStruct((n, CONV_CH), F32),
            jax.ShapeDtypeStruct((n, ATTN_WIDTH), BF16),
        ],
        input_output_aliases=aliases,
        compiler_params=_cparams(1),
        name="inproj",
    )(*args)


def _lambda_value(lam_ref, lam_init):
    lp = lam_ref[...]
    t1 = jnp.sum(lp[0:1] * lp[1:2], axis=1, keepdims=True)
    t2 = jnp.sum(lp[2:3] * lp[3:4], axis=1, keepdims=True)
    return jnp.exp(t1) - jnp.exp(t2) + lam_init


def _subln(o, g, lam_init):
    ms = jnp.mean(o * o, axis=-1, keepdims=True)
    return o * lax.rsqrt(ms + EPS) * g * (1.0 - lam_init)


def _pattn_kernel(pt_ref, q_ref, k_ref, v_ref, lam_ref, sgc_ref, q8_ref, qt_ref, kn_ref, vn_ref, sg_ref, ck_hbm, cv_hbm,
                  o_ref, od_ref, vt_scr, qs_scr, m_scr, l_scr, acc_scr, kbuf, vbuf, sem,
                  *, tq, layer, steps, seq0, n_pages, page, per_step, lam_init):
    i = pl.program_id(1)
    step = pl.program_id(0) * pl.num_programs(1) + i
    pk_refs, pv_refs = _sample_pages(pt_ref, step, steps, seq0, ck_hbm, cv_hbm, kbuf, vbuf, sem,
                                     layer=layer, n_pages=n_pages, per_step=per_step)
    nkv = k_ref.shape[0] // tq

    @pl.when(i == 0)
    def _():
        for c in range(nkv):
            for hd in range(N_HEADS):
                rows = v_ref[pl.ds(c * tq * N_HEADS + hd, tq, stride=N_HEADS), :]
                vt_scr[c, hd * V_DIM:(hd + 1) * V_DIM, :] = rows.T.astype(BF16)

    lam = _lambda_value(lam_ref, lam_init)
    lane = lax.broadcasted_iota(jnp.int32, (tq, LANES), 1)
    causal = (lax.broadcasted_iota(jnp.int32, (tq, tq), 0) <= lax.broadcasted_iota(jnp.int32, (tq, tq), 1))

    for sh in range(N_SUB):
        qp = q_ref[:, (sh // 2) * LANES:(sh // 2 + 1) * LANES]
        keep = (lane < HEAD_DIM) if sh % 2 == 0 else (lane >= HEAD_DIM)
        qs_scr[sh] = jnp.where(keep, qp, jnp.zeros_like(qp))
        m_scr[sh] = jnp.full((1, tq), -jnp.inf, F32)
        l_scr[sh] = jnp.zeros((1, tq), F32)
        acc_scr[sh] = jnp.zeros((V_DIM, tq), F32)

    def step(j, masked):
        off = pl.multiple_of(j * tq, tq)
        hq = tq // 2
        chains = [(sh, qh) for sh in range(N_SUB) for qh in range(2)]

        def nkeys(qh):
            return hq if (masked and qh == 0) else tq

        def scores(sh, qh):
            kj = k_ref[pl.ds(off, nkeys(qh)), (sh // 2) * LANES:(sh // 2 + 1) * LANES]
            return lax.dot_general(kj, qs_scr[sh, qh * hq:(qh + 1) * hq, :], (((1,), (1,)), ((), ())),
                                   preferred_element_type=F32)

        pending = [scores(*chains[c]) for c in range(QK_AHEAD)]
        for c, (sh, qh) in enumerate(chains):
            sl = slice((sh // 2) * LANES, (sh // 2 + 1) * LANES)
            cols = slice(qh * hq, (qh + 1) * hq)
            nk = nkeys(qh)
            st = pending.pop(0)
            if c + QK_AHEAD < len(chains):
                pending.append(scores(*chains[c + QK_AHEAD]))
            if masked:
                st = jnp.where(causal[0:nk, cols], st, -jnp.inf)
            m_prev = m_scr[sh, :, cols]
            m_new = jnp.maximum(m_prev, jnp.max(st, axis=0, keepdims=True))
            alpha = jnp.exp2(m_prev - m_new)
            p = jnp.exp2(st - m_new)
            l_scr[sh, :, cols] = alpha * l_scr[sh, :, cols] + jnp.sum(p, axis=0, keepdims=True)
            acc_scr[sh, :, cols] = alpha * acc_scr[sh, :, cols] + jnp.dot(vt_scr[j, sl, 0:nk], p.astype(BF16),
                                                                          preferred_element_type=F32)
            m_scr[sh, :, cols] = m_new

    def body(j, carry):
        step(j, False)
        return carry

    lax.fori_loop(0, i, body, 0)
    step(i, True)

    sgc = sgc_ref[...]
    for hd in range(N_HEADS):
        od = acc_scr[2 * hd] / l_scr[2 * hd] - lam * (acc_scr[2 * hd + 1] / l_scr[2 * hd + 1])
        ms = jnp.mean(od * od, axis=0, keepdims=True)
        on = od * lax.rsqrt(ms + EPS) * sgc * (1.0 - lam_init)
        o_ref[:, hd * LANES:(hd + 1) * LANES] = on.T.astype(BF16)
    _sample_attend(q8_ref, qt_ref, kn_ref, vn_ref, lam, sg_ref[...], pk_refs, pv_refs, od_ref,
                   n_pages=n_pages, page=page, per_step=per_step, lam_init=lam_init)


def _prompt_attention(q, kb, v_all, layer, lam_p, sgc, batch, seq, tq,
                      page_table, seq0, n_seq, q8, qt, kn8, vn8, sg, ckt, cvf, lam_init):
    nq = seq // tq
    n = batch * seq
    steps = batch * nq
    n_pages = page_table.shape[1]
    page = ckt.shape[-1]
    per_step = n_seq // steps
    assert per_step * steps == n_seq
    seq3 = lambda b, i, pt: (b * nq + i, 0, 0)
    const2 = lambda b, i, pt: (0, 0)
    kern = functools.partial(_pattn_kernel, tq=tq, layer=layer, steps=steps, seq0=seq0, n_pages=n_pages, page=page,
                             per_step=per_step, lam_init=lam_init)
    grid_spec = pltpu.PrefetchScalarGridSpec(
        num_scalar_prefetch=1,
        grid=(batch, nq),
        in_specs=[
            pl.BlockSpec((tq, ATTN_WIDTH), lambda b, i, pt: (b * nq + i, 0)),
            pl.BlockSpec((seq, ATTN_WIDTH), lambda b, i, pt: (b, 0)),
            pl.BlockSpec((None, seq * N_HEADS, V_DIM), lambda b, i, pt: (layer, b, 0)),
            pl.BlockSpec((4, HEAD_DIM), const2),
            pl.BlockSpec((V_DIM, 1), const2),
        ] + _sample_specs(per_step, seq3, const2),
        out_specs=[pl.BlockSpec((tq, ATTN_WIDTH), lambda b, i, pt: (b * nq + i, 0)),
                   pl.BlockSpec((per_step, N_HEADS, V_DIM), seq3)],
        scratch_shapes=[
            pltpu.VMEM((nq, ATTN_WIDTH, tq), BF16),
            pltpu.VMEM((N_SUB, tq, LANES), BF16),
            pltpu.VMEM((N_SUB, 1, tq), F32),
            pltpu.VMEM((N_SUB, 1, tq), F32),
            pltpu.VMEM((N_SUB, V_DIM, tq), F32),
        ] + _sample_scratch(per_step, n_pages, page),
    )
    return pl.pallas_call(
        kern,
        grid_spec=grid_spec,
        out_shape=[jax.ShapeDtypeStruct((n, ATTN_WIDTH), BF16), jax.ShapeDtypeStruct((n_seq, N_HEADS, V_DIM), F32)],
        compiler_params=_cparams(2),
        name="prompt_attn",
    )(page_table, q, kb, v_all, lam_p, sgc, q8, qt, kn8, vn8, sg, ckt, cvf)


def _decode_scores(q8, qt, kn, k_refs, page):
    qcols = [jnp.broadcast_to(qt[:, h:h + 1], (HEAD_DIM, page)) for h in range(N_SUB)]
    s = jnp.concatenate(
        [jnp.concatenate([jnp.sum(k_ref[h] * qcols[h], axis=0, keepdims=True) for h in range(N_SUB)], axis=0)
         for k_ref in k_refs], axis=1)
    s_self = jnp.sum(q8 * kn, axis=1, keepdims=True)
    return s, s_self


def _decode_probs(s, s_self, lam):
    m = jnp.maximum(jnp.max(s, axis=1, keepdims=True), s_self)
    pexp = jnp.exp2(s - m)
    pself = jnp.exp2(s_self - m)
    inv = 1.0 / (jnp.sum(pexp, axis=1, keepdims=True) + pself)
    sub = lax.broadcasted_iota(jnp.int32, (N_SUB, 1), 0)
    coef = jnp.where((sub & 1) == 0, 1.0, -lam) * inv
    a8 = pexp * coef
    a8 = (a8 + pltpu.roll(a8, N_SUB - 1, 0)).astype(BF16)
    as8 = jnp.broadcast_to(pself * coef, (N_SUB, V_DIM))
    return a8, as8 + pltpu.roll(as8, N_SUB - 1, 0)


def _decode_values(a8, as8, vn, sg, v_refs, page, lam_init):
    o_self = as8 * vn
    rows = []
    for hd in range(N_HEADS):
        v_hd = jnp.concatenate([v_ref[pl.ds(hd, page, stride=N_HEADS), :].astype(BF16) for v_ref in v_refs], axis=0)
        acc = jnp.dot(a8, v_hd, preferred_element_type=F32)
        rows.append(acc[2 * hd:2 * hd + 1] + o_self[2 * hd:2 * hd + 1])
    o = jnp.concatenate(rows, axis=0)
    return _subln(o, sg, lam_init)


def _sample_pages(pt_ref, step, steps, seq0, ck_hbm, cv_hbm, kbuf, vbuf, sem, *, layer, n_pages, per_step):
    nkv = per_step * n_pages

    def page_copy(slot, j, phys):
        return (pltpu.make_async_copy(ck_hbm.at[layer, phys], kbuf.at[slot, j], sem.at[slot]),
                pltpu.make_async_copy(cv_hbm.at[layer, phys], vbuf.at[slot, j], sem.at[slot]))

    def start_pages(st, slot):
        for j in range(nkv):
            for cp in page_copy(slot, j, pt_ref[seq0 + st * per_step + j // n_pages, j % n_pages]):
                cp.start()

    slot = step % 2

    @pl.when(step == 0)
    def _():
        start_pages(0, 0)

    for j in range(nkv):
        for cp in page_copy(slot, j, 0):
            cp.wait()

    @pl.when(step + 1 < steps)
    def _():
        start_pages(step + 1, 1 - slot)

    return [kbuf.at[slot, j] for j in range(nkv)], [vbuf.at[slot, j] for j in range(nkv)]


def _sample_attend(q8_ref, qt_ref, kn_ref, vn_ref, lam, sg, k_refs, v_refs, od_ref, *, n_pages, page, per_step, lam_init):
    pages = [slice(sq * n_pages, (sq + 1) * n_pages) for sq in range(per_step)]
    scores = [_decode_scores(q8_ref[sq], qt_ref[sq], kn_ref[sq], k_refs[pages[sq]], page) for sq in range(per_step)]
    probs = [_decode_probs(*scores[sq], lam) for sq in range(per_step)]
    for sq in range(per_step):
        od_ref[sq] = _decode_values(*probs[sq], vn_ref[sq], sg, v_refs[pages[sq]], page, lam_init)


def _sample_scratch(per_step, n_pages, page):
    return [pltpu.VMEM((2, per_step * n_pages, N_SUB, HEAD_DIM, page), F32),
            pltpu.VMEM((2, per_step * n_pages, page * N_HEADS, V_DIM), F32),
            pltpu.SemaphoreType.DMA((2,))]


def _sample_specs(per_step, seq3, const2):
    return [pl.BlockSpec((per_step, N_SUB, HEAD_DIM), seq3), pl.BlockSpec((per_step, HEAD_DIM, N_SUB), seq3),
            pl.BlockSpec((per_step, N_SUB, HEAD_DIM), seq3), pl.BlockSpec((per_step, N_SUB, V_DIM), seq3),
            pl.BlockSpec((1, V_DIM), const2),
            pl.BlockSpec(memory_space=pl.ANY), pl.BlockSpec(memory_space=pl.ANY)]


def _conv_post(y, lg_ref, lb_ref):
    mu = jnp.mean(y, axis=-1, keepdims=True)
    yc = y - mu
    var = jnp.mean(yc * yc, axis=-1, keepdims=True)
    yn = yc * lax.rsqrt(var + EPS) * lg_ref[...] + lb_ref[...]
    return yn * jax.nn.sigmoid(yn)


def _mix_tail(x, att, cnv, wo_ref, g2_ref, w1_ref, w2_ref, xo_ref, h_ref, gate_ref, grow_ref=None):
    y = (x + jnp.dot(att, wo_ref[0:ATTN_WIDTH, :], preferred_element_type=F32)
         + jnp.dot(cnv.astype(BF16), wo_ref[ATTN_WIDTH:, :], preferred_element_type=F32))
    xo_ref[...] = y
    ms = jnp.mean(y * y, axis=-1, keepdims=True)
    h = y * lax.rsqrt(ms + EPS) * g2_ref[...]
    h_hi = h.astype(BF16)
    h_lo = (h - h_hi.astype(F32)).astype(BF16)
    h_ref[...] = h_hi
    r = jnp.dot(h_hi, w1_ref[...], preferred_element_type=F32)
    lg = r + pltpu.roll(r, LANES // 2, 1) + jnp.dot(h_lo, w2_ref[...], preferred_element_type=F32)
    tm = lg.shape[0]
    lane = lax.broadcasted_iota(jnp.int32, (tm, LANES), 1)
    lanef = lane.astype(F32)
    big = float(LANES)
    neg = -jnp.inf
    gmask = lane < N_GROUPS
    gl = jnp.where(gmask, lg, neg)
    gmax = jnp.max(gl, axis=1, keepdims=True)
    gsum = jnp.sum(jnp.where(gmask, jnp.exp(gl - gmax), 0.0), axis=1, keepdims=True)
    g_w = 1.0 / gsum
    g_idx = jnp.min(jnp.where(gl == gmax, lanef, big), axis=1, keepdims=True)
    lo = N_GROUPS + EXP_PER_GROUP * g_idx
    emask = (lanef >= lo) & (lanef < lo + EXP_PER_GROUP)
    ev = jnp.where(emask, lg, neg)
    v1 = jnp.max(ev, axis=1, keepdims=True)
    i1 = jnp.min(jnp.where(ev == v1, lanef, big), axis=1, keepdims=True)
    ev2 = jnp.where(lanef == i1, neg, ev)
    v2 = jnp.max(ev2, axis=1, keepdims=True)
    i2 = jnp.min(jnp.where(ev2 == v2, lanef, big), axis=1, keepdims=True)
    e2 = jnp.exp(v2 - v1)
    den = 1.0 / (1.0 + e2)
    w1 = den * g_w
    w2 = e2 * den * g_w
    gates = jnp.where(lanef == i1, w1, jnp.where(lanef == i2, w2, 0.0))
    gates = pltpu.roll(gates, LANES - N_GROUPS, 1)
    gate_ref[...] = jnp.where(lane == GROUP_LANE, g_idx, gates)
    if grow_ref is not None:
        grow_ref[...] = jnp.transpose(jnp.broadcast_to(g_idx, (tm, LANES)))[0:SUBLANES, :]


def _mix_prompt_kernel(pt_ref, x_ref, att_ref, u_ref, uh_ref, cw_ref, cb_ref, lg_ref, lb_ref, wo_ref, g2_ref,
                       w1_ref, w2_ref, lam_ref, q8_ref, qt_ref, kn_ref, vn_ref, sg_ref, ck_hbm, cv_hbm,
                       xo_ref, h_ref, gate_ref, grow_ref, od_ref, ext_scr, kbuf, vbuf, sem,
                       *, tm, tiles_per_seq, layer, steps, seq0, n_pages, page, per_step, lam_init):
    i = pl.program_id(0)
    k_refs, v_refs = _sample_pages(pt_ref, i, steps, seq0, ck_hbm, cv_hbm, kbuf, vbuf, sem,
                                   layer=layer, n_pages=n_pages, per_step=per_step)
    first = (i % tiles_per_seq) == 0
    ext_scr[0:HALO, :] = jnp.where(first, 0.0, uh_ref[...])
    ext_scr[HALO:HALO + tm, :] = u_ref[...]
    ext_scr[HALO + tm:, :] = jnp.zeros((EXT_TAIL, CONV_CH), F32)
    lead = HALO - CONV_BUF
    acc = jnp.zeros((tm, CONV_CH), F32) + cb_ref[...]
    for r in range(SUBLANES):
        z = None
        for a in range((CONV_W + lead + SUBLANES - 1) // SUBLANES):
            j = SUBLANES * a + r - lead
            if 0 <= j < CONV_W:
                term = cw_ref[j:j + 1, :] * ext_scr[SUBLANES * a:SUBLANES * a + tm + SUBLANES, :]
                z = term if z is None else z + term
        acc = acc + z[r:r + tm]
    cnv = _conv_post(acc, lg_ref, lb_ref)
    _mix_tail(x_ref[...], att_ref[...], cnv, wo_ref, g2_ref, w1_ref, w2_ref, xo_ref, h_ref, gate_ref, grow_ref)
    _sample_attend(q8_ref, qt_ref, kn_ref, vn_ref, _lambda_value(lam_ref, lam_init), sg_ref[...], k_refs, v_refs, od_ref,
                   n_pages=n_pages, page=page, per_step=per_step, lam_init=lam_init)


def _mix_decode_kernel(x_ref, att_ref, u_ref, st_ref, cw_ref, cb_ref, lg_ref, lb_ref, wo_ref, g2_ref,
                       w1_ref, w2_ref, xo_ref, h_ref, gate_ref, ns_ref):
    u = u_ref[...]
    acc = cb_ref[...] + cw_ref[CONV_BUF:CONV_BUF + 1, :] * u
    for j in range(CONV_BUF):
        acc = acc + cw_ref[j:j + 1, :] * st_ref[j]
    for j in range(CONV_BUF - 1):
        ns_ref[j] = st_ref[j + 1]
    ns_ref[CONV_BUF - 1] = u
    cnv = _conv_post(acc, lg_ref, lb_ref)
    _mix_tail(x_ref[...], att_ref[...], cnv, wo_ref, g2_ref, w1_ref, w2_ref, xo_ref, h_ref, gate_ref)


def _mix_common_specs(tm, layer):
    row = lambda i, *_: (i, 0)
    full = lambda i, *_: (0, 0)
    tail_in = [
        pl.BlockSpec((HALO, CONV_CH), full),
        pl.BlockSpec((1, CONV_CH), full),
        pl.BlockSpec((1, CONV_CH), full),
        pl.BlockSpec((1, CONV_CH), full),
        pl.BlockSpec((None, D_MODEL, D_MODEL), lambda i, *_: (layer, 0, 0)),
        pl.BlockSpec((1, D_MODEL), full),
        pl.BlockSpec((D_MODEL, LANES), full),
        pl.BlockSpec((D_MODEL, LANES), full),
    ]
    out_specs = [pl.BlockSpec((tm, D_MODEL), row), pl.BlockSpec((tm, D_MODEL), row),
                 pl.BlockSpec((tm, LANES), row)]
    return row, tail_in, out_specs


def _mix_out_shape(n):
    return [jax.ShapeDtypeStruct((n, D_MODEL), F32), jax.ShapeDtypeStruct((n, D_MODEL), BF16),
            jax.ShapeDtypeStruct((n, LANES), F32)]


def _mix_prompt(x, att, u, layer, cwp, cb, lg, lb, wo_b, g2, w1, w2, seq, tm,
                page_table, seq0, n_seq, q8, qt, kn8, vn8, lam_p, sg, ckt, cvf, lam_init):
    n = x.shape[0]
    steps = n // tm
    n_pages = page_table.shape[1]
    page = ckt.shape[-1]
    per_step = n_seq // steps
    assert per_step * steps == n_seq
    row, tail_in, out_specs = _mix_common_specs(tm, layer)
    halo_map = lambda i, pt: (jnp.maximum(i * (tm // HALO) - 1, 0), 0)
    seq3 = lambda i, pt: (i, 0, 0)
    const2 = lambda i, pt: (0, 0)
    in_specs = [pl.BlockSpec((tm, D_MODEL), row), pl.BlockSpec((tm, ATTN_WIDTH), row),
                pl.BlockSpec((tm, CONV_CH), row), pl.BlockSpec((HALO, CONV_CH), halo_map)] + tail_in
    in_specs += [pl.BlockSpec((4, HEAD_DIM), const2)] + _sample_specs(per_step, seq3, const2)
    kern = functools.partial(_mix_prompt_kernel, tm=tm, tiles_per_seq=seq // tm, layer=layer, steps=steps, seq0=seq0,
                             n_pages=n_pages, page=page, per_step=per_step, lam_init=lam_init)
    grid_spec = pltpu.PrefetchScalarGridSpec(
        num_scalar_prefetch=1,
        grid=(steps,),
        in_specs=in_specs,
        out_specs=out_specs + [pl.BlockSpec((SUBLANES, tm), lambda i, pt: (0, i)),
                               pl.BlockSpec((per_step, N_HEADS, V_DIM), seq3)],
        scratch_shapes=[pltpu.VMEM((tm + HALO + EXT_TAIL, CONV_CH), F32)] + _sample_scratch(per_step, n_pages, page),
    )
    return pl.pallas_call(
        kern,
        grid_spec=grid_spec,
        out_shape=_mix_out_shape(n) + [jax.ShapeDtypeStruct((SUBLANES, n), F32),
                                       jax.ShapeDtypeStruct((n_seq, N_HEADS, V_DIM), F32)],
        compiler_params=_cparams(1),
        name="mix_prompt",
    )(page_table, x, att, u, u, cwp, cb, lg, lb, wo_b, g2, w1, w2, lam_p, q8, qt, kn8, vn8, sg, ckt, cvf)


def _mix_decode(x, att, u, state_t, layer, cwp, cb, lg, lb, wo_b, g2, w1, w2, tm):
    n = x.shape[0]
    row, tail_in, out_specs = _mix_common_specs(tm, layer)
    return pl.pallas_call(
        _mix_decode_kernel,
        grid=(n // tm,),
        in_specs=[pl.BlockSpec((tm, D_MODEL), row), pl.BlockSpec((tm, ATTN_WIDTH), row),
                  pl.BlockSpec((tm, CONV_CH), row),
                  pl.BlockSpec((None, CONV_BUF, tm, CONV_CH), lambda i: (layer, 0, i, 0))] + tail_in,
        out_specs=out_specs + [pl.BlockSpec((CONV_BUF, tm, CONV_CH), lambda i: (0, i, 0))],
        out_shape=_mix_out_shape(n) + [jax.ShapeDtypeStruct((CONV_BUF, n, CONV_CH), F32)],
        compiler_params=_cparams(1),
        name="mix_decode",
    )(x, att, u, state_t, cwp, cb, lg, lb, wo_b, g2, w1, w2)


def _moe_kernel(x_ref, h_ref, gate_ref, wg_ref, wu_ref, wd_ref, o_ref):
    e = pl.program_id(1)

    @pl.when(e == 0)
    def _():
        o_ref[...] = x_ref[...]

    h = h_ref[...]
    hg = jnp.dot(h, wg_ref[...], preferred_element_type=F32)
    hu = jnp.dot(h, wu_ref[...], preferred_element_type=F32)
    gates = gate_ref[...]
    lane = lax.broadcasted_iota(jnp.int32, gates.shape, 1)
    ge = jnp.sum(jnp.where(lane == e, gates, 0.0), axis=1, keepdims=True)
    act = (hg * jax.nn.sigmoid(hg)) * hu * ge
    o_ref[...] += jnp.dot(act.astype(BF16), wd_ref[...], preferred_element_type=F32)


def _moe(x, h, gates, layer, wg_b, wu_b, wd_b, tm):
    n = x.shape[0]
    row = lambda i, e: (i, 0)
    return pl.pallas_call(
        _moe_kernel,
        grid=(n // tm, N_EXPERTS),
        in_specs=[
            pl.BlockSpec((tm, D_MODEL), row),
            pl.BlockSpec((tm, D_MODEL), row),
            pl.BlockSpec((tm, LANES), row),
            pl.BlockSpec((None, None, D_MODEL, D_EXPERT), lambda i, e: (layer, e, 0, 0)),
            pl.BlockSpec((None, None, D_MODEL, D_EXPERT), lambda i, e: (layer, e, 0, 0)),
            pl.BlockSpec((None, None, D_EXPERT, D_MODEL), lambda i, e: (layer, e, 0, 0)),
        ],
        out_specs=pl.BlockSpec((tm, D_MODEL), row),
        out_shape=jax.ShapeDtypeStruct((n, D_MODEL), F32),
        compiler_params=_cparams(2),
        name="moe",
    )(x, h, gates, wg_b, wu_b, wd_b)


def _moe_sparse_kernel(x_ref, h_ref, gate_ref, grow_ref, wg_ref, wu_ref, wd_ref, o_ref, lt_scr, dcol_scr, drow_scr,
                       cnt_scr, *, rp):
    i = pl.program_id(0)
    g = pl.program_id(1)
    tb = h_ref.shape[0]

    @pl.when((i == 0) & (g == 0))
    def _():
        r = lax.broadcasted_iota(jnp.int32, (tb, tb), 0)
        c = lax.broadcasted_iota(jnp.int32, (tb, tb), 1)
        lt_scr[...] = jnp.where(c < r, 1.0, 0.0).astype(BF16)

    @pl.when(g == 0)
    def _():
        o_ref[...] = x_ref[...]

    gf = g.astype(F32)
    gates = gate_ref[...]
    lane = lax.broadcasted_iota(jnp.int32, gates.shape, 1)
    gcol = jnp.sum(jnp.where(lane == GROUP_LANE, gates, 0.0), axis=1, keepdims=True)
    grow = grow_ref[0:1, :]

    @pl.when(g == 0)
    def _():
        lt = lt_scr[...]
        oh_c = jnp.where(gcol == lane.astype(F32), 1.0, 0.0)
        rk_c = jnp.dot(lt, oh_c.astype(BF16), preferred_element_type=F32)
        dcol_scr[...] = jnp.sum(oh_c * rk_c, axis=1, keepdims=True)
        sub = lax.broadcasted_iota(jnp.int32, (SUBLANES, tb), 0).astype(F32)
        oh_r = jnp.where(grow == sub, 1.0, 0.0)
        rk_r = lax.dot_general(oh_r.astype(BF16), lt, (((1,), (1,)), ((), ())), preferred_element_type=F32)
        drow_scr[...] = jnp.broadcast_to(jnp.sum(oh_r * rk_r, axis=0, keepdims=True), (SUBLANES, tb))
        for l in range(N_GROUPS):
            cnt_scr[l] = jnp.sum(oh_r[l:l + 1, :]).astype(jnp.int32)

    dcol = jnp.where(gcol == gf, dcol_scr[...], -1.0)
    drow = jnp.where(grow == gf, drow_scr[0:1, :], -1.0)
    cnt = cnt_scr[g]
    half = rp // 2
    rem = cnt % rp
    nbig = cnt // rp + (rem > half).astype(jnp.int32)

    g_hi = gates.astype(BF16)
    g_lo = (gates - g_hi.astype(F32)).astype(BF16)
    h = h_ref[...]

    def chunk(first_slot, rows):
        base = first_slot.astype(F32)
        lane_r = lax.broadcasted_iota(jnp.int32, (rows, LANES), 1)
        riota = lax.broadcasted_iota(jnp.int32, (rows, 1), 0).astype(F32)
        ciota = lax.broadcasted_iota(jnp.int32, (1, rows), 1).astype(F32)
        pk = jnp.where(drow == riota + base, 1.0, 0.0).astype(BF16)
        xs = jnp.dot(pk, h, preferred_element_type=F32).astype(BF16)
        gs = (jnp.dot(pk, g_hi, preferred_element_type=F32)
              + jnp.dot(pk, g_lo, preferred_element_type=F32))
        y = jnp.zeros((rows, o_ref.shape[1]), F32)
        for e in range(EXP_PER_GROUP):
            hg = jnp.dot(xs, wg_ref[e], preferred_element_type=F32)
            hu = jnp.dot(xs, wu_ref[e], preferred_element_type=F32)
            ge = jnp.sum(jnp.where(lane_r == g * EXP_PER_GROUP + e, gs, 0.0), axis=1, keepdims=True)
            act = (hg * jax.nn.sigmoid(hg)) * hu * ge
            y = y + jnp.dot(act.astype(BF16), wd_ref[e], preferred_element_type=F32)
        ptk = jnp.where(dcol == ciota + base, 1.0, 0.0).astype(BF16)
        o_ref[...] += jnp.dot(ptk, y.astype(BF16), preferred_element_type=F32)

    def big(k, carry):
        chunk(k * rp, rp)
        return carry

    lax.fori_loop(0, nbig, big, 0)

    @pl.when((rem > 0) & (rem <= half))
    def _():
        chunk(nbig * rp, half)


def _moe_sparse(x, h, gates, grow, layer, wg_b, wu_b, wd_b, tb, rp):
    n, d = x.shape
    de = wg_b.shape[-1]
    row = lambda i, g: (i, 0)
    kern = functools.partial(_moe_sparse_kernel, rp=rp)
    return pl.pallas_call(
        kern,
        grid=(n // tb, N_GROUPS),
        in_specs=[
            pl.BlockSpec((tb, d), row),
            pl.BlockSpec((tb, d), row),
            pl.BlockSpec((tb, LANES), row),
            pl.BlockSpec((SUBLANES, tb), lambda i, g: (0, i)),
            pl.BlockSpec((None, EXP_PER_GROUP, d, de), lambda i, g: (layer, g, 0, 0)),
            pl.BlockSpec((None, EXP_PER_GROUP, d, de), lambda i, g: (layer, g, 0, 0)),
            pl.BlockSpec((None, EXP_PER_GROUP, de, d), lambda i, g: (layer, g, 0, 0)),
        ],
        out_specs=pl.BlockSpec((tb, d), row),
        out_shape=jax.ShapeDtypeStruct((n, d), F32),
        scratch_shapes=[pltpu.VMEM((tb, tb), BF16), pltpu.VMEM((tb, 1), F32), pltpu.VMEM((SUBLANES, tb), F32),
                        pltpu.SMEM((N_GROUPS,), jnp.int32)],
        compiler_params=_cparams(2),
        name="moe_sparse",
    )(x, h, gates, grow, wg_b, wu_b, wd_b)


def _rope_tables(pos):
    half = ROT_DIM // 2
    inv = jnp.power(ROPE_THETA, -jnp.arange(0, ROT_DIM, 2, dtype=F32) / ROT_DIM)
    ang = pos.astype(F32)[:, None] * inv[None, :]
    cos, sin = jnp.cos(ang), jnp.sin(ang)
    n = pos.shape[0]
    pad = jnp.zeros((n, HEAD_DIM - ROT_DIM), F32)
    zer = jnp.zeros((n, half), F32)
    c = jnp.concatenate([cos, cos, pad + 1.0], axis=1)
    s1 = jnp.concatenate([-sin, zer, pad], axis=1)
    s2 = jnp.concatenate([zer, sin, pad], axis=1)
    rep = LANES // HEAD_DIM
    return jnp.tile(c, (1, rep)), jnp.tile(s1, (1, rep)), jnp.tile(s2, (1, rep))


def kernel(x_prompt, x_sample, cache_k, cache_v, state_conv, page_table, norm1_g, w_in, q_norm_g, k_norm_g,
           lam_q1, lam_k1, lam_q2, lam_k2, subln_g, conv_w, conv_b, conv_ln_g, conv_ln_b, w_out, norm2_g,
           w_router_group, w_router_expert, w_gate, w_up, w_down):
    batch, seq, d = x_prompt.shape
    db = x_sample.shape[0]
    depth = w_in.shape[0]
    n_pages, page = page_table.shape[1], cache_k.shape[2]
    past = n_pages * page
    n = batch * seq

    xp = x_prompt.reshape(n, d)
    xs = x_sample.reshape(db, d)
    tabs_p = _rope_tables(jnp.arange(seq, dtype=jnp.int32))
    tabs_s = _rope_tables(jnp.full((db,), past, dtype=jnp.int32))
    gi = jnp.arange(ATTN_WIDTH, dtype=jnp.int32) // HEAD_DIM
    bd = (gi[:, None] == gi[None, :]).astype(BF16)
    ckt = jnp.transpose(cache_k, (0, 1, 3, 4, 2))
    cvf = cache_v.reshape(depth, cache_v.shape[1], page * N_HEADS, V_DIM)
    state_t = jnp.transpose(state_conv, (0, 2, 1, 3))
    w_in_b, wo_b = w_in.astype(BF16), w_out.astype(BF16)
    wg_b, wu_b, wd_b = w_gate.astype(BF16), w_up.astype(BF16), w_down.astype(BF16)
    kvp = (jnp.zeros((depth, batch, ATTN_WIDTH, seq), F32), jnp.zeros((depth, n * N_HEADS, V_DIM), F32))
    kvs = (jnp.zeros((depth, 1, ATTN_WIDTH, db), F32), jnp.zeros((depth, db * N_HEADS, V_DIM), F32))

    cp_l, cs_l = [], []
    for l in range(depth):
        lam_init = 0.8 - 0.6 * math.exp(-0.3 * l)
        g1 = norm1_g[l][None, :]
        qg = jnp.tile(q_norm_g[l], N_SUB)[None, :]
        kg = jnp.tile(k_norm_g[l], N_SUB)[None, :]
        lam_p = jnp.stack([lam_q1[l], lam_k1[l], lam_q2[l], lam_k2[l]])
        sg = subln_g[l][None, :]
        cwp = jnp.pad(conv_w[l], ((0, HALO - CONV_W), (0, 0)))
        cb, lg, lb = conv_b[l][None, :], conv_ln_g[l][None, :], conv_ln_b[l][None, :]
        g2 = norm2_g[l][None, :]
        wr = jnp.concatenate([w_router_group[l], w_router_expert[l]], axis=1)
        wr_hi = wr.astype(BF16)
        wr_lo = (wr - wr_hi.astype(F32)).astype(BF16)
        nr = wr.shape[1]
        w1 = jnp.zeros((d, LANES), BF16).at[:, :nr].set(wr_hi).at[:, LANES // 2:LANES // 2 + nr].set(wr_lo)
        w2 = jnp.zeros((d, LANES), BF16).at[:, :nr].set(wr_hi)

        q, k_all, v_all, u, kb = _inproj(xp, l, depth, kvp, g1, w_in_b, bd, qg, kg, *tabs_p, tm=TM_PROJ)
        kvp = (k_all, v_all)
        qs, ks_all, vs_all, us, _ = _inproj(xs, l, depth, kvs, g1, w_in_b, bd, qg, kg, *tabs_s, tm=db)
        kvs = (ks_all, vs_all)
        q8 = qs.astype(F32).reshape(db, N_SUB, HEAD_DIM)
        vn8 = jnp.repeat(vs_all[l].reshape(db, N_HEADS, V_DIM), 2, axis=1)
        kn8 = jnp.transpose(ks_all[l, 0].reshape(N_SUB, HEAD_DIM, db), (2, 0, 1))

        qt = jnp.transpose(q8, (0, 2, 1))
        n_mix = n // TM_MIX
        att, att_s2 = _prompt_attention(q, kb, v_all, l, lam_p, sg.reshape(V_DIM, 1), batch, seq, TQ_ATTN, page_table,
                                        n_mix, db - n_mix, q8[n_mix:], qt[n_mix:], kn8[n_mix:], vn8[n_mix:], sg, ckt,
                                        cvf, lam_init)
        xp, h2, gates, grow, att_s1 = _mix_prompt(xp, att, u, l, cwp, cb, lg, lb, wo_b, g2, w1, w2, seq, TM_MIX,
                                                  page_table, 0, n_mix, q8[:n_mix], qt[:n_mix], kn8[:n_mix],
                                                  vn8[:n_mix], lam_p, sg, ckt, cvf, lam_init)
        att_s = jnp.concatenate([att_s1, att_s2], axis=0)
        xp = _moe_sparse(xp, h2, gates, grow, l, wg_b, wu_b, wd_b, TB_MOE, RP_MOE)
        cp_l.append(u.reshape(batch, seq, CONV_CH)[:, seq - CONV_BUF:])

        att_s = att_s.reshape(db, ATTN_WIDTH).astype(BF16)
        xs, h2, gates, ns = _mix_decode(xs, att_s, us, state_t, l, cwp, cb, lg, lb, wo_b, g2, w1, w2, TM_MIX_SAMPLE)
        xs = _moe(xs, h2, gates, l, wg_b, wu_b, wd_b, db)
        cs_l.append(jnp.transpose(ns, (1, 0, 2)))

    k_prompt = jnp.transpose(kvp[0].reshape(depth, batch, N_SUB, HEAD_DIM, seq), (0, 1, 4, 2, 3))
    k_sample = jnp.transpose(kvs[0].reshape(depth, 1, N_SUB, HEAD_DIM, db), (0, 4, 1, 2, 3))
    return (xp.reshape(batch, seq, d), xs.reshape(db, 1, d),
            k_prompt, kvp[1].reshape(depth, batch, seq, N_HEADS, V_DIM), jnp.stack(cp_l),
            k_sample, kvs[1].reshape(depth, db, 1, N_HEADS, V_DIM), jnp.stack(cs_l))
```
